```python
import math
import jax, jax.numpy as jnp
from jax import lax
import numpy as np

D_MODEL = 1024
BATCH = 8
SEQ = 2048
DEPTH = 2
DEC_BATCH = 8
DEC_SEQ = 4096
PAST_LEN = 128

M_HEADS = 4
M_QK_DIM = 256
M_V_DIM = 256
M_CHUNK = 128
M_IN = 2 * M_HEADS * M_QK_DIM + 2 * M_HEADS * M_V_DIM + 4 * M_HEADS
A_HEADS = 16
A_KV_HEADS = 4
A_GROUP = A_HEADS // A_KV_HEADS
A_HEAD_DIM = 64
WINDOW = 128
A_BLOCK = 128
A_SPAN = A_BLOCK + 2 * WINDOW
A_IN = A_HEADS * A_HEAD_DIM + 2 * A_KV_HEADS * A_HEAD_DIM
NUM_BUCKETS = 32
MAX_DISTANCE = 128
D_FF = 2816
CONV_WIDTH = 3
EPS = 1e-6
N_MIXERS = 2
N_MLSTM_LAYERS = (DEPTH + 1) // 2
N_ATTN_LAYERS = DEPTH // 2

kernel_name = 'hybrid_mlstm_swa_convffn_encoder'


def rmsnorm(x, g):
    xf = x.astype(jnp.float32)
    y = xf * lax.rsqrt(jnp.mean(xf * xf, axis=-1, keepdims=True) + EPS)
    return (y * g.astype(jnp.float32)).astype(x.dtype)


def mlstm_scan(q, k, v, ig, lf):
    B, H, S, DK = q.shape
    DV = v.shape[-1]
    nc = S // M_CHUNK
    chunk = lambda t: jnp.moveaxis(t.reshape(B, H, nc, M_CHUNK, *t.shape[3:]), 2, 0)
    xs = (chunk(q), chunk(k), chunk(v), chunk(ig), chunk(lf))
    tril = jnp.tril(jnp.ones((M_CHUNK, M_CHUNK), dtype=bool))

    def step(carry, inp):
        C, n, m = carry
        qc, kc, vc, igc, lfc = inp
        b = jnp.cumsum(lfc, axis=-1)
        log_d = b[..., :, None] - b[..., None, :] + igc[..., None, :]
        log_d = jnp.where(tril, log_d, -jnp.inf)
        m_inter = b + m[..., None]
        m_t = jnp.maximum(jnp.max(log_d, axis=-1), m_inter)
        s = jnp.einsum('bhtd,bhsd->bhts', qc, kc) * jnp.exp(log_d - m_t[..., None])
        w_inter = jnp.exp(m_inter - m_t)[..., None]
        num = jnp.einsum('bhts,bhse->bhte', s, vc) + w_inter * jnp.einsum('bhtd,bhde->bhte', qc, C)
        den = jnp.sum(s, axis=-1, keepdims=True) + w_inter * jnp.einsum('bhtd,bhd->bht', qc, n)[..., None]
        h = num / jnp.maximum(jnp.abs(den), jnp.exp(-m_t)[..., None])
        b_last = b[..., -1]
        log_w = b_last[..., None] - b + igc
        m_new = jnp.maximum(b_last + m, jnp.max(log_w, axis=-1))
        decay = jnp.exp(b_last + m - m_new)
        w = jnp.exp(log_w - m_new[..., None])
        C_new = decay[..., None, None] * C + jnp.einsum('bhs,bhsd,bhse->bhde', w, kc, vc)
        n_new = decay[..., None] * n + jnp.einsum('bhs,bhsd->bhd', w, kc)
        return (C_new, n_new, m_new), h

    init = (jnp.zeros((B, H, DK, DV), jnp.float32), jnp.zeros((B, H, DK), jnp.float32),
            jnp.zeros((B, H), jnp.float32))
    _, hs = lax.scan(step, init, xs)
    return jnp.moveaxis(hs, 0, 2).reshape(B, H, S, DV)


def mlstm_mixer(h, w_in, b_gate, head_g, w_out):
    B, S, _ = h.shape
    qk = M_HEADS * M_QK_DIM
    vw = M_HEADS * M_V_DIM
    proj = h @ w_in
    q = proj[..., :qk]
    k = proj[..., qk:2 * qk]
    v = proj[..., 2 * qk:2 * qk + vw]
    o = proj[..., 2 * qk + vw:2 * qk + 2 * vw]
    gates = (proj[..., 2 * qk + 2 * vw:] + b_gate).astype(jnp.float32).reshape(B, S, 2, 2, M_HEADS)
    gates = jnp.transpose(gates, (2, 3, 0, 4, 1))
    heads = lambda t, d: jnp.transpose(t.reshape(B, S, M_HEADS, d), (0, 2, 1, 3)).astype(jnp.float32)
    qh = heads(q, M_QK_DIM) * (M_QK_DIM ** -0.5)
    kh = heads(k, M_QK_DIM)
    vh = heads(v, M_V_DIM)
    h_fwd = mlstm_scan(qh, kh, vh, gates[0, 0], jax.nn.log_sigmoid(gates[0, 1]))
    rev = lambda t: jnp.flip(t, axis=2)
    h_bwd = rev(mlstm_scan(rev(qh), rev(kh), rev(vh), rev(gates[1, 0]),
                           rev(jax.nn.log_sigmoid(gates[1, 1]))))
    hs = h_fwd + h_bwd
    hs = hs * lax.rsqrt(jnp.mean(hs * hs, axis=-1, keepdims=True) + EPS) * head_g.astype(jnp.float32)[None, :, None, :]
    hs = jnp.transpose(hs, (0, 2, 1, 3)).reshape(B, S, vw).astype(h.dtype)
    return (hs * jax.nn.sigmoid(o)) @ w_out


def t5_bucket(rel):
    nb = NUM_BUCKETS // 2
    max_exact = nb // 2
    ret = jnp.where(rel > 0, nb, 0)
    n = jnp.abs(rel)
    nf = jnp.maximum(n, 1).astype(jnp.float32)
    large = max_exact + (jnp.log(nf / max_exact) / math.log(MAX_DISTANCE / max_exact)
                         * (nb - max_exact)).astype(jnp.int32)
    large = jnp.minimum(large, nb - 1)
    return ret + jnp.where(n < max_exact, n, large)


def swa_mixer(h, w_in, sink, rel_bias, w_out):
    B, S, _ = h.shape
    qw = A_HEADS * A_HEAD_DIM
    kw = A_KV_HEADS * A_HEAD_DIM
    proj = h @ w_in
    q = proj[..., :qw].reshape(B, S, A_KV_HEADS, A_GROUP, A_HEAD_DIM)
    k = proj[..., qw:qw + kw].reshape(B, S, A_KV_HEADS, A_HEAD_DIM)
    v = proj[..., qw + kw:].reshape(B, S, A_KV_HEADS, A_HEAD_DIM)
    k_pad = jnp.pad(k, ((0, 0), (WINDOW, WINDOW), (0, 0), (0, 0)))
    v_pad = jnp.pad(v, ((0, 0), (WINDOW, WINDOW), (0, 0), (0, 0)))
    a_idx = jnp.arange(A_BLOCK)[:, None]
    c_idx = jnp.arange(A_SPAN)[None, :]
    rel = c_idx - WINDOW - a_idx
    band = jnp.abs(rel) <= WINDOW
    bias = jnp.transpose(rel_bias.astype(jnp.float32)[t5_bucket(rel)], (2, 0, 1))
    bias = bias.reshape(A_KV_HEADS, A_GROUP, A_BLOCK, A_SPAN)
    sink_l = sink.astype(jnp.float32).reshape(1, A_KV_HEADS, A_GROUP, 1, 1)
    scale = A_HEAD_DIM ** -0.5

    def block(i):
        qb = lax.dynamic_slice_in_dim(q, i * A_BLOCK, A_BLOCK, axis=1)
        kb = lax.dynamic_slice_in_dim(k_pad, i * A_BLOCK, A_SPAN, axis=1)
        vb = lax.dynamic_slice_in_dim(v_pad, i * A_BLOCK, A_SPAN, axis=1)
        s = jnp.einsum('bqkgd,bckd->bkgqc', qb, kb).astype(jnp.float32) * scale + bias
        key_pos = i * A_BLOCK - WINDOW + jnp.arange(A_SPAN)
        valid = band & ((key_pos >= 0) & (key_pos < S))[None, :]
        s = jnp.where(valid, s, -jnp.inf)
        m = jnp.maximum(jnp.max(s, axis=-1, keepdims=True), sink_l)
        p = jnp.exp(s - m)
        denom = jnp.sum(p, axis=-1, keepdims=True) + jnp.exp(sink_l - m)
        ob = jnp.einsum('bkgqc,bckd->bqkgd', p / denom, vb.astype(jnp.float32))
        return ob.astype(h.dtype)

    outs = lax.map(block, jnp.arange(S // A_BLOCK))
    outs = jnp.moveaxis(outs, 0, 1).reshape(B, S, qw)
    return outs @ w_out


def conv_ffn(h, w_up, conv_w, conv_b, w_down):
    u = h @ w_up
    u = lax.conv_general_dilated(u, conv_w[:, None, :], window_strides=(1,),
                                 padding=((CONV_WIDTH // 2, CONV_WIDTH // 2),),
                                 dimension_numbers=('NWC', 'WIO', 'NWC'),
                                 feature_group_count=2 * D_FF) + conv_b
    g = u[..., :D_FF]
    val = u[..., D_FF:]
    return (jax.nn.silu(g) * val) @ w_down


def trunk(x, c, adaln_w, adaln_b, norm_g, mlstm_w_in, mlstm_b_gate, mlstm_head_g, mlstm_w_out,
          attn_w_in, attn_sink, attn_w_out, rel_bias, ffn_w_up, ffn_conv_w, ffn_conv_b, ffn_w_down,
          final_g):
    for l in range(DEPTH):
        mod = jax.nn.silu(c) @ adaln_w[l] + adaln_b[l]
        sh1, sc1, g1, sh2, sc2, g2 = [t[:, None, :] for t in jnp.split(mod, 6, axis=-1)]
        h = rmsnorm(x, norm_g[l, 0]) * (1 + sc1) + sh1
        if l % N_MIXERS == 0:
            j = l // N_MIXERS
            y = mlstm_mixer(h, mlstm_w_in[j], mlstm_b_gate[j], mlstm_head_g[j], mlstm_w_out[j])
        else:
            j = l // N_MIXERS
            y = swa_mixer(h, attn_w_in[j], attn_sink[j], rel_bias, attn_w_out[j])
        x = x + g1 * y
        h = rmsnorm(x, norm_g[l, 1]) * (1 + sc2) + sh2
        x = x + g2 * conv_ffn(h, ffn_w_up[l], ffn_conv_w[l], ffn_conv_b[l], ffn_w_down[l])
    return rmsnorm(x, final_g)


def setup_inputs(seed: int = 0) -> dict:
    key = jax.random.key(seed)
    ks = jax.random.split(key, 24)
    nrm = lambda k, shape, s: jax.random.normal(k, shape, jnp.float32) * s
    d = D_MODEL
    ig_bias = nrm(ks[6], (N_MLSTM_LAYERS, 2, 1, M_HEADS), 0.1)
    f_bias = 3.0 + 3.0 * jax.random.uniform(ks[7], (N_MLSTM_LAYERS, 2, 1, M_HEADS), jnp.float32)
    b_gate = jnp.concatenate([ig_bias, f_bias], axis=2).reshape(N_MLSTM_LAYERS, 4 * M_HEADS)
    return {
        'x_prompt': nrm(ks[0], (BATCH, SEQ, d), 1.0),
        'x_sample': nrm(ks[1], (DEC_BATCH, DEC_SEQ, d), 1.0),
        'c_prompt': nrm(ks[2], (BATCH, d), 1.0),
        'c_sample': nrm(ks[3], (DEC_BATCH, d), 1.0),
        'adaln_w': nrm(ks[4], (DEPTH, d, 6 * d), 0.5 * d ** -0.5),
        'adaln_b': nrm(ks[5], (DEPTH, 6 * d), 0.02),
        'norm_g': 1.0 + nrm(ks[8], (DEPTH, 2, d), 0.05),
        'mlstm_w_in': nrm(ks[9], (N_MLSTM_LAYERS, d, M_IN), d ** -0.5),
        'mlstm_b_gate': b_gate,
        'mlstm_head_g': 1.0 + nrm(ks[10], (N_MLSTM_LAYERS, M_HEADS, M_V_DIM), 0.05),
        'mlstm_w_out': nrm(ks[11], (N_MLSTM_LAYERS, M_HEADS * M_V_DIM, d), (M_HEADS * M_V_DIM) ** -0.5),
        'attn_w_in': nrm(ks[12], (N_ATTN_LAYERS, d, A_IN), d ** -0.5),
        'attn_sink': nrm(ks[13], (N_ATTN_LAYERS, A_HEADS), 0.5),
        'attn_w_out': nrm(ks[14], (N_ATTN_LAYERS, A_HEADS * A_HEAD_DIM, d), (A_HEADS * A_HEAD_DIM) ** -0.5),
        'rel_bias': nrm(ks[15], (NUM_BUCKETS, A_HEADS), 0.5),
        'ffn_w_up': nrm(ks[16], (DEPTH, d, 2 * D_FF), d ** -0.5),
        'ffn_conv_w': nrm(ks[17], (DEPTH, CONV_WIDTH, 2 * D_FF), CONV_WIDTH ** -0.5),
        'ffn_conv_b': nrm(ks[18], (DEPTH, 2 * D_FF), 0.02),
        'ffn_w_down': nrm(ks[19], (DEPTH, D_FF, d), D_FF ** -0.5),
        'final_g': 1.0 + nrm(ks[20], (d,), 0.05),
    }


def reference(x_prompt, x_sample, c_prompt, c_sample, adaln_w, adaln_b, norm_g, mlstm_w_in,
              mlstm_b_gate, mlstm_head_g, mlstm_w_out, attn_w_in, attn_sink, attn_w_out, rel_bias,
              ffn_w_up, ffn_conv_w, ffn_conv_b, ffn_w_down, final_g):
    y_prompt = trunk(x_prompt, c_prompt, adaln_w, adaln_b, norm_g, mlstm_w_in, mlstm_b_gate,
                     mlstm_head_g, mlstm_w_out, attn_w_in, attn_sink, attn_w_out, rel_bias,
                     ffn_w_up, ffn_conv_w, ffn_conv_b, ffn_w_down, final_g)
    y_sample = trunk(x_sample, c_sample, adaln_w, adaln_b, norm_g, mlstm_w_in, mlstm_b_gate,
                     mlstm_head_g, mlstm_w_out, attn_w_in, attn_sink, attn_w_out, rel_bias,
                     ffn_w_up, ffn_conv_w, ffn_conv_b, ffn_w_down, final_g)
    return (y_prompt, y_sample)
```

```python
import functools
import math

import jax
import jax.numpy as jnp
from jax import lax
from jax.experimental import pallas as pl
from jax.experimental.pallas import tpu as pltpu

F32 = jnp.float32
BF16 = jnp.bfloat16

D_MODEL = 1024
DEPTH = 2
M_HEADS = 4
M_DK = 256
M_DV = 256
A_HEADS = 16
A_KV = 4
A_GROUP = A_HEADS // A_KV
A_HD = 64
WINDOW = 128
A_BLOCK = 128
A_SPAN = A_BLOCK + 2 * WINDOW
NUM_BUCKETS = 32
MAX_DISTANCE = 128
D_FF = 2816
EPS = 1e-6

LANES = 128
MXU_DIM = 256
VMEM_LIMIT = 56 * 1024 * 1024

SCAN_CHUNK = 128
GATE_PAD = LANES
HALO = 16
FFN_CHUNK = MXU_DIM


def _cparams(n_axes):
    return pltpu.CompilerParams(dimension_semantics=("parallel",) * n_axes,
                                vmem_limit_bytes=VMEM_LIMIT)


def _resident(shape):
    nd = len(shape)
    return pl.BlockSpec(shape, lambda *_: (0,) * nd, pipeline_mode=pl.Buffered(1))


def _norm_mod(x, g, sc, sh):
    ms = jnp.mean(x * x, axis=-1, keepdims=True)
    y = x * lax.rsqrt(ms + EPS) * g
    return y * (1.0 + sc) + sh


def _adaln_kernel(c_ref, w_ref, b_ref, o_ref):
    c = c_ref[...]
    s = c * jax.nn.sigmoid(c)
    o_ref[0] = jnp.dot(s, w_ref[0], precision=lax.Precision.HIGHEST,
                       preferred_element_type=F32) + b_ref[0]


def _adaln(c_all, adaln_w, adaln_b):
    nb = c_all.shape[0]
    n = adaln_w.shape[-1]
    tn = 1536
    return pl.pallas_call(
        _adaln_kernel,
        grid=(DEPTH, n // tn),
        in_specs=[pl.BlockSpec((nb, D_MODEL), lambda l, j: (0, 0)),
                  pl.BlockSpec((1, D_MODEL, tn), lambda l, j: (l, 0, j)),
                  pl.BlockSpec((1, 1, tn), lambda l, j: (l, 0, j))],
        out_specs=pl.BlockSpec((1, nb, tn), lambda l, j: (l, 0, j)),
        out_shape=jax.ShapeDtypeStruct((DEPTH, nb, n), F32),
        compiler_params=_cparams(2),
        name="adaln",
    )(c_all, adaln_w, adaln_b.reshape(DEPTH, 1, n))


def _inproj_kernel(x_ref, g_ref, sc_ref, sh_ref, w_ref, *rest, tn, with_gates):
    if with_gates:
        wg_ref, o_ref, og_ref = rest
    else:
        (o_ref,) = rest
    h = _norm_mod(x_ref[0], g_ref[...], sc_ref[0, 0], sh_ref[0, 0]).astype(BF16)
    n = w_ref.shape[1]
    for j in range(n // tn):
        o_ref[0, :, j * tn:(j + 1) * tn] = jnp.dot(
            h, w_ref[:, j * tn:(j + 1) * tn], preferred_element_type=F32).astype(BF16)
    if with_gates:
        og_ref[0] = jnp.dot(h, wg_ref[...], preferred_element_type=F32)


def _inproj(x, g, mod4, sc_idx, sh_idx, w, w_gate=None, *, tm):
    bsz, s, d = x.shape
    n = w.shape[1]
    with_gates = w_gate is not None
    in_specs = [pl.BlockSpec((1, tm, d), lambda b, i: (b, i, 0)),
                _resident((1, d)),
                pl.BlockSpec((1, 1, 1, d), lambda b, i: (b, sc_idx, 0, 0)),
                pl.BlockSpec((1, 1, 1, d), lambda b, i: (b, sh_idx, 0, 0)),
                _resident((d, n))]
    args = [x, g.reshape(1, d), mod4, mod4, w]
    out_specs = [pl.BlockSpec((1, tm, n), lambda b, i: (b, i, 0))]
    out_shape = [jax.ShapeDtypeStruct((bsz, s, n), BF16)]
    if with_gates:
        in_specs.append(_resident((d, GATE_PAD)))
        args.append(w_gate)
        out_specs.append(pl.BlockSpec((1, tm, GATE_PAD), lambda b, i: (b, i, 0)))
        out_shape.append(jax.ShapeDtypeStruct((bsz, s, GATE_PAD), F32))
    return pl.pallas_call(
        functools.partial(_inproj_kernel, tn=512, with_gates=with_gates),
        grid=(bsz, s // tm),
        in_specs=in_specs, out_specs=out_specs, out_shape=out_shape,
        compiler_params=_cparams(2),
        name="inproj_gates" if with_gates else "inproj",
    )(*args)


def _lane_scan(x, op, ident, reverse, lane):
    n = x.shape[-1]
    k = 1
    while k < n:
        if reverse:
            shifted = jnp.where(lane < n - k, pltpu.roll(x, n - k, axis=1), ident)
        else:
            shifted = jnp.where(lane >= k, pltpu.roll(x, k, axis=1), ident)
        x = op(x, shifted)
        k *= 2
    return x


def _gateprep_kernel(g_ref, b_ref, p_ref, *, nc, chunk):
    lane = lax.broadcasted_iota(jnp.int32, (M_HEADS, chunk), 1)
    row8 = lax.broadcasted_iota(jnp.int32, (8, chunk), 0)

    def body(c, carry):
        r0 = pl.multiple_of(c * chunk, chunk)
        gt = (g_ref[0, pl.ds(r0, chunk), :] + b_ref[...]).T
        for d in range(2):
            ig = gt[d * 8:d * 8 + 4]
            fg = gt[d * 8 + 4:d * 8 + 8]
            lf = jnp.minimum(fg, 0.0) - jnp.log1p(jnp.exp(-jnp.abs(fg)))
            bsum = _lane_scan(lf, jnp.add, 0.0, d == 1, lane)
            beta = ig - bsum
            cmax = _lane_scan(beta, jnp.maximum, -jnp.inf, d == 1, lane)
            last = 0 if d == 1 else chunk - 1
            b_last = jnp.max(jnp.where(lane == last, bsum, -jnp.inf), axis=1, keepdims=True)
            g_max = jnp.max(beta, axis=1, keepdims=True)
            rows = (beta, cmax, bsum,
                    jnp.broadcast_to(b_last, beta.shape), jnp.broadcast_to(g_max, beta.shape))
            for h in range(M_HEADS):
                tile = jnp.zeros((8, chunk), F32)
                for r, val in enumerate(rows):
                    tile = jnp.where(row8 == r, jnp.broadcast_to(val[h:h + 1], (8, chunk)), tile)
                p_ref[0, c, h, d * 8:(d + 1) * 8, :] = tile
        return carry

    lax.fori_loop(0, nc, body, 0)


def _gateprep(gates, b_gate, chunk):
    bsz, s, _ = gates.shape
    nc = s // chunk
    bias = jnp.zeros((1, GATE_PAD), F32).at[0, :4 * M_HEADS].set(b_gate)
    return pl.pallas_call(
        functools.partial(_gateprep_kernel, nc=nc, chunk=chunk),
        grid=(bsz,),
        in_specs=[pl.BlockSpec((1, s, GATE_PAD), lambda b: (b, 0, 0)),
                  _resident((1, GATE_PAD))],
        out_specs=pl.BlockSpec((1, nc, M_HEADS, 16, chunk), lambda b: (b, 0, 0, 0, 0)),
        out_shape=jax.ShapeDtypeStruct((bsz, nc, M_HEADS, 16, chunk), F32),
        compiler_params=_cparams(1),
        name="gateprep",
    )(gates, bias)


def _mlstm_kernel(q_ref, k_ref, v_ref, o_ref, p_ref, hg_ref, a_ref, acc, c_scr, n_scr,
                  *, nc, chunk):
    L = chunk
    acc[...] = jnp.zeros_like(acc)
    c_scr[...] = jnp.zeros_like(c_scr)
    n_scr[...] = jnp.zeros_like(n_scr)
    row = lax.broadcasted_iota(jnp.int32, (L, L), 0)
    col = lax.broadcasted_iota(jnp.int32, (L, L), 1)
    tri = (col <= row, col >= row)
    reps = M_DV // L

    def colb(x):
        return jnp.broadcast_to(x, (L, L)).T

    def widen(x):
        return x if reps == 1 else jnp.concatenate([x] * reps, axis=1)

    def step(c, d, m):
        r0 = pl.multiple_of(c * L, L)
        q = q_ref[0, pl.ds(r0, L), :]
        k = k_ref[0, pl.ds(r0, L), :]
        v = v_ref[0, pl.ds(r0, L), :]
        t = p_ref[0, c, 0, d * 8:(d + 1) * 8, :]
        beta, cmax, bsum, b_last, g_max = (t[i:i + 1] for i in range(5))
        mx = jnp.maximum(cmax, m)
        w_inter = jnp.exp(m - mx)
        neg_mt = jnp.exp(-(bsum + mx))
        m_c = jnp.maximum(m, g_max)
        w = jnp.exp(beta - m_c)
        decay = jnp.exp(m - m_c)
        m_new = b_last + m_c

        dmat = jnp.where(tri[d], jnp.exp(colb(-mx) + beta), 0.0)
        qk = lax.dot_general(q, k, (((1,), (1,)), ((), ())), preferred_element_type=F32)
        s = qk * dmat
        c_old = c_scr[d]
        n_old = n_scr[d]
        wi_col = colb(w_inter)
        num = (jnp.dot(s.astype(BF16), v, preferred_element_type=F32)
               + widen(wi_col) * jnp.dot(q, c_old.astype(BF16), preferred_element_type=F32))
        qn = jnp.sum(q.astype(F32) * n_old, axis=-1, keepdims=True)
        den = jnp.sum(s, axis=-1, keepdims=True) + wi_col[:, :1] * qn
        hout = num / jnp.maximum(jnp.abs(den), colb(neg_mt)[:, :1])
        acc[pl.ds(r0, L), :] += hout

        ks = k.astype(F32) * widen(colb(w))
        dec = widen(decay)
        c_scr[d] = dec * c_old + lax.dot_general(
            ks.astype(BF16), v, (((0,), (0,)), ((), ())), preferred_element_type=F32)
        n_scr[d] = dec * n_old + jnp.sum(ks, axis=0, keepdims=True)
        return m_new

    def body(i, carry):
        mf, mb = carry
        return step(i, 0, mf), step(nc - 1 - i, 1, mb)

    m0 = jnp.zeros((1, L), F32)
    lax.fori_loop(0, nc, body, (m0, m0))

    def fin(i, carry):
        r0 = pl.multiple_of(i * L, L)
        hs = acc[pl.ds(r0, L), :]
        ms = jnp.mean(hs * hs, axis=-1, keepdims=True)
        y = hs * lax.rsqrt(ms + EPS) * hg_ref[0]
        gate = jax.nn.sigmoid(o_ref[0, pl.ds(r0, L), :].astype(F32))
        a_ref[0, pl.ds(r0, L), :] = (y * gate).astype(BF16)
        return carry

    lax.fori_loop(0, nc, fin, 0)


def _mlstm_scan(qkvo, prep, head_g, chunk):
    bsz, s, _ = qkvo.shape
    nc = s // chunk
    blk = lambda off: pl.BlockSpec((1, s, M_DK), lambda b, h: (b, 0, off + h))
    return pl.pallas_call(
        functools.partial(_mlstm_kernel, nc=nc, chunk=chunk),
        grid=(bsz, M_HEADS),
        in_specs=[blk(0), blk(M_HEADS), blk(2 * M_HEADS), blk(3 * M_HEADS),
                  pl.BlockSpec((1, nc, 1, 16, chunk), lambda b, h: (b, 0, h, 0, 0)),
                  pl.BlockSpec((1, 1, M_DV), lambda b, h: (h, 0, 0))],
        out_specs=pl.BlockSpec((1, s, M_DV), lambda b, h: (b, 0, h)),
        out_shape=jax.ShapeDtypeStruct((bsz, s, M_HEADS * M_DV), BF16),
        scratch_shapes=[pltpu.VMEM((s, M_DV), F32),
                        pltpu.VMEM((2, M_DK, M_DV), F32),
                        pltpu.VMEM((2, 1, M_DK), F32)],
        compiler_params=_cparams(2),
        name="mlstm_scan",
    )(qkvo, qkvo, qkvo, qkvo, prep, head_g.reshape(M_HEADS, 1, M_DV))


def _outproj_kernel(a_ref, w_ref, x_ref, g_ref, o_ref):
    y = jnp.dot(a_ref[0], w_ref[...], preferred_element_type=F32)
    o_ref[0] = x_ref[0] + g_ref[0, 0] * y


def _outproj(a, w, x, mod4, g_idx, *, tm):
    bsz, s, d = x.shape
    kdim = a.shape[-1]
    return pl.pallas_call(
        _outproj_kernel,
        grid=(bsz, s // tm),
        in_specs=[pl.BlockSpec((1, tm, kdim), lambda b, i: (b, i, 0)),
                  _resident((kdim, d)),
                  pl.BlockSpec((1, tm, d), lambda b, i: (b, i, 0)),
                  pl.BlockSpec((1, 1, 1, d), lambda b, i: (b, g_idx, 0, 0))],
        out_specs=pl.BlockSpec((1, tm, d), lambda b, i: (b, i, 0)),
        out_shape=jax.ShapeDtypeStruct((bsz, s, d), F32),
        compiler_params=_cparams(2),
        name="outproj",
    )(a, w, x, mod4)


def _ffn_kernel(x_ref, xp_ref, xn_ref, g_ref, sc_ref, sh_ref, gate_ref, wup_ref, cw_ref, cb_ref,
                wdn_ref, fg_ref, o_ref, h_scr, a_scr, *, tm, final):
    i = pl.program_id(1)
    last = pl.num_programs(1) - 1
    g = g_ref[...]
    sc = sc_ref[0, 0]
    sh = sh_ref[0, 0]
    x = x_ref[0]
    hp = jnp.where(i > 0, _norm_mod(xp_ref[0], g, sc, sh), 0.0)
    hn = jnp.where(i < last, _norm_mod(xn_ref[0], g, sc, sh), 0.0)
    h_scr[0:HALO] = hp.astype(BF16)
    h_scr[HALO:HALO + tm] = _norm_mod(x, g, sc, sh).astype(BF16)
    h_scr[HALO + tm:] = hn.astype(BF16)
    h = h_scr[...]
    ext = tm + 2 * HALO

    def conv(u, off):
        w0 = cw_ref[0:1, off:off + FFN_CHUNK]
        w1 = cw_ref[1:2, off:off + FFN_CHUNK]
        w2 = cw_ref[2:3, off:off + FFN_CHUNK]
        prev = pltpu.roll(u, 1, axis=0)
        nxt = pltpu.roll(u, ext - 1, axis=0)
        out = w0 * prev + w1 * u + w2 * nxt + cb_ref[:, off:off + FFN_CHUNK]
        return out[HALO:HALO + tm]

    for f in range(D_FF // FFN_CHUNK):
        og = f * FFN_CHUNK
        ov = D_FF + f * FFN_CHUNK
        ug = conv(jnp.dot(h, wup_ref[:, og:og + FFN_CHUNK], preferred_element_type=F32), og)
        uv = conv(jnp.dot(h, wup_ref[:, ov:ov + FFN_CHUNK], preferred_element_type=F32), ov)
        a_scr[:, og:og + FFN_CHUNK] = (ug * jax.nn.sigmoid(ug) * uv).astype(BF16)

    y = jnp.dot(a_scr[...], wdn_ref[...], preferred_element_type=F32)
    out = x + gate_ref[0, 0] * y
    if final:
        ms = jnp.mean(out * out, axis=-1, keepdims=True)
        out = out * lax.rsqrt(ms + EPS) * fg_ref[...]
    o_ref[0] = out


def _ffn(x, g, mod4, w_up, conv_w, conv_b, w_down, final_g, *, tm, final):
    bsz, s, d = x.shape
    hb = tm // HALO
    nh = s // HALO
    return pl.pallas_call(
        functools.partial(_ffn_kernel, tm=tm, final=final),
        grid=(bsz, s // tm),
        in_specs=[pl.BlockSpec((1, tm, d), lambda b, i: (b, i, 0)),
                  pl.BlockSpec((1, HALO, d), lambda b, i: (b, jnp.maximum(i * hb - 1, 0), 0)),
                  pl.BlockSpec((1, HALO, d), lambda b, i: (b, jnp.minimum((i + 1) * hb, nh - 1), 0)),
                  _resident((1, d)),
                  pl.BlockSpec((1, 1, 1, d), lambda b, i: (b, 4, 0, 0)),
                  pl.BlockSpec((1, 1, 1, d), lambda b, i: (b, 3, 0, 0)),
                  pl.BlockSpec((1, 1, 1, d), lambda b, i: (b, 5, 0, 0)),
                  _resident((d, 2 * D_FF)),
                  _resident((3, 2 * D_FF)),
                  _resident((1, 2 * D_FF)),
                  _resident((D_FF, d)),
                  _resident((1, d))],
        out_specs=pl.BlockSpec((1, tm, d), lambda b, i: (b, i, 0)),
        out_shape=jax.ShapeDtypeStruct((bsz, s, d), F32),
        scratch_shapes=[pltpu.VMEM((tm + 2 * HALO, d), BF16),
                        pltpu.VMEM((tm, D_FF), BF16)],
        compiler_params=_cparams(2),
        name="ffn_final" if final else "ffn",
    )(x, x, x, g.reshape(1, d), mod4, mod4, mod4, w_up, conv_w, conv_b.reshape(1, -1), w_down,
      final_g.reshape(1, d))


def _t5_bucket(rel):
    nb = NUM_BUCKETS // 2
    max_exact = nb // 2
    ret = jnp.where(rel > 0, nb, 0)
    n = jnp.abs(rel)
    nf = jnp.maximum(n, 1).astype(jnp.float32)
    large = max_exact + (jnp.log(nf / max_exact) / math.log(MAX_DISTANCE / max_exact)
                         * (nb - max_exact)).astype(jnp.int32)
    large = jnp.minimum(large, nb - 1)
    return ret + jnp.where(n < max_exact, n, large)


def _bias_kernel(bucket_ref, rb_ref, o_ref):
    hh = pl.program_id(0)
    head = (hh % A_KV) * A_GROUP + hh // A_KV
    bucket = bucket_ref[...]
    q = lax.broadcasted_iota(jnp.int32, bucket.shape, 0)
    c = lax.broadcasted_iota(jnp.int32, bucket.shape, 1)
    band = jnp.abs(c - WINDOW - q) <= WINDOW
    val = jnp.zeros(bucket.shape, F32)
    for kb in range(NUM_BUCKETS):
        val = jnp.where(bucket == kb, rb_ref[kb, head], val)
    o_ref[0] = jnp.where(band, val, -jnp.inf)


def _bias_table(rel_bias):
    a_idx = jnp.arange(A_BLOCK)[:, None]
    c_idx = jnp.arange(A_SPAN)[None, :]
    bucket = _t5_bucket(c_idx - WINDOW - a_idx).astype(jnp.int32)
    return pl.pallas_call(
        _bias_kernel,
        grid=(A_HEADS,),
        in_specs=[pl.BlockSpec((A_BLOCK, A_SPAN), lambda h: (0, 0)),
                  pl.BlockSpec(memory_space=pltpu.SMEM)],
        out_specs=pl.BlockSpec((1, A_BLOCK, A_SPAN), lambda h: (h, 0, 0)),
        out_shape=jax.ShapeDtypeStruct((A_HEADS, A_BLOCK, A_SPAN), F32),
        compiler_params=_cparams(1),
        name="bias_table",
    )(bucket, rel_bias)


def _swa_kernel(q_ref, kp_ref, kc_ref, kn_ref, vp_ref, vc_ref, vn_ref, bias_ref, sink_ref, o_ref):
    i = pl.program_id(1)
    last = pl.num_programs(1) - 1
    kcat = jnp.concatenate([kp_ref[0], kc_ref[0], kn_ref[0]], axis=0)
    vcat = jnp.concatenate([vp_ref[0], vc_ref[0], vn_ref[0]], axis=0)
    colk = lax.broadcasted_iota(jnp.int32, (1, A_SPAN), 1)
    edge = ((colk < WINDOW) & (i == 0)) | ((colk >= WINDOW + A_BLOCK) & (i == last))
    lane = lax.broadcasted_iota(jnp.int32, (A_BLOCK, A_KV * A_HD), 1)
    for g in range(A_GROUP):
        qg = q_ref[0, :, g * MXU_DIM:(g + 1) * MXU_DIM]
        qm = jnp.concatenate(
            [jnp.where(lane // A_HD == kv, qg, jnp.zeros_like(qg)) for kv in range(A_KV)], axis=0)
        s_all = lax.dot_general(qm, kcat, (((1,), (1,)), ((), ())), preferred_element_type=F32)
        ps = []
        rden = []
        for kv in range(A_KV):
            s = s_all[kv * A_BLOCK:(kv + 1) * A_BLOCK] + bias_ref[g * A_KV + kv]
            s = jnp.where(edge, -jnp.inf, s)
            sink = sink_ref[kv * A_GROUP + g]
            m = jnp.maximum(jnp.max(s, axis=-1, keepdims=True), sink)
            p = jnp.exp(s - m)
            den = jnp.sum(p, axis=-1, keepdims=True) + jnp.exp(sink - m)
            ps.append(p.astype(BF16))
            rden.append(1.0 / den)
        pv = jnp.dot(jnp.concatenate(ps, axis=0), vcat, preferred_element_type=F32)
        out = jnp.zeros((A_BLOCK, A_KV * A_HD), F32)
        for kv in range(A_KV):
            out = jnp.where(lane // A_HD == kv, pv[kv * A_BLOCK:(kv + 1) * A_BLOCK] * rden[kv], out)
        o_ref[0, :, g * MXU_DIM:(g + 1) * MXU_DIM] = out.astype(BF16)


def _swa(qkv, bias_tbl, sink):
    bsz, s, _ = qkv.shape
    nb = s // A_BLOCK
    kcol = A_HEADS * A_HD // MXU_DIM
    vcol = kcol + 1
    prev = lambda b, i: jnp.maximum(i - 1, 0)
    nxt = lambda b, i: jnp.minimum(i + 1, nb - 1)
    kv_spec = lambda col, row: pl.BlockSpec((1, A_BLOCK, MXU_DIM), lambda b, i: (b, row(b, i), col))
    cur = lambda b, i: i
    return pl.pallas_call(
        _swa_kernel,
        grid=(bsz, nb),
        in_specs=[pl.BlockSpec((1, A_BLOCK, A_HEADS * A_HD), lambda b, i: (b, i, 0)),
                  kv_spec(kcol, prev), kv_spec(kcol, cur), kv_spec(kcol, nxt),
                  kv_spec(vcol, prev), kv_spec(vcol, cur), kv_spec(vcol, nxt),
                  _resident((A_HEADS, A_BLOCK, A_SPAN)),
                  pl.BlockSpec(memory_space=pltpu.SMEM)],
        out_specs=pl.BlockSpec((1, A_BLOCK, A_HEADS * A_HD), lambda b, i: (b, i, 0)),
        out_shape=jax.ShapeDtypeStruct((bsz, s, A_HEADS * A_HD), BF16),
        compiler_params=_cparams(2),
        name="swa",
    )(qkv, qkv, qkv, qkv, qkv, qkv, qkv, bias_tbl, sink)


def _prep_weights(mlstm_w_in, mlstm_w_out, attn_w_in, attn_w_out, ffn_w_up, ffn_w_down):
    qk = M_HEADS * M_DK
    w_m = mlstm_w_in[0]
    w_main = jnp.concatenate([w_m[:, :qk] * (M_DK ** -0.5), w_m[:, qk:4 * qk]], axis=1).astype(BF16)
    w_gate = jnp.zeros((D_MODEL, GATE_PAD), F32).at[:, :4 * M_HEADS].set(w_m[:, 4 * qk:]).astype(BF16)
    qw = A_HEADS * A_HD
    w_a = attn_w_in[0]
    wq = w_a[:, :qw].reshape(D_MODEL, A_KV, A_GROUP, A_HD).transpose(0, 2, 1, 3).reshape(D_MODEL, qw)
    w_attn = jnp.concatenate([wq * (A_HD ** -0.5), w_a[:, qw:]], axis=1).astype(BF16)
    wo_attn = attn_w_out[0].reshape(A_KV, A_GROUP, A_HD, D_MODEL).transpose(1, 0, 2, 3)
    wo_attn = wo_attn.reshape(qw, D_MODEL).astype(BF16)
    return dict(w_main=w_main, w_gate=w_gate, wo_m=mlstm_w_out[0].astype(BF16),
                w_attn=w_attn, wo_attn=wo_attn,
                w_up=ffn_w_up.astype(BF16), w_down=ffn_w_down.astype(BF16))


def _trunk(x, mod, wts, bias_tbl, norm_g, mlstm_b_gate, mlstm_head_g, attn_sink, ffn_conv_w,
           ffn_conv_b, final_g, *, tm):
    bsz, s, d = x.shape
    mod4 = mod[0].reshape(bsz, 6, 1, d)
    qkvo, gates = _inproj(x, norm_g[0, 0], mod4, 1, 0, wts["w_main"], wts["w_gate"], tm=tm)
    prep = _gateprep(gates, mlstm_b_gate[0], SCAN_CHUNK)
    a = _mlstm_scan(qkvo, prep, mlstm_head_g[0], SCAN_CHUNK)
    x = _outproj(a, wts["wo_m"], x, mod4, 2, tm=tm)
    x = _ffn(x, norm_g[0, 1], mod4, wts["w_up"][0], ffn_conv_w[0], ffn_conv_b[0], wts["w_down"][0],
             final_g, tm=tm, final=False)
    mod4 = mod[1].reshape(bsz, 6, 1, d)
    (qkv,) = _inproj(x, norm_g[1, 0], mod4, 1, 0, wts["w_attn"], tm=tm)
    a = _swa(qkv, bias_tbl, attn_sink[0])
    x = _outproj(a, wts["wo_attn"], x, mod4, 2, tm=tm)
    return _ffn(x, norm_g[1, 1], mod4, wts["w_up"][1], ffn_conv_w[1], ffn_conv_b[1], wts["w_down"][1],
                final_g, tm=tm, final=True)


def kernel(x_prompt, x_sample, c_prompt, c_sample, adaln_w, adaln_b, norm_g, mlstm_w_in, mlstm_b_gate, mlstm_head_g, mlstm_w_out, attn_w_in, attn_sink, attn_w_out, rel_bias, ffn_w_up, ffn_conv_w, ffn_conv_b, ffn_w_down, final_g):
    nbp = x_prompt.shape[0]
    mod = _adaln(jnp.concatenate([c_prompt, c_sample], axis=0), adaln_w, adaln_b)
    wts = _prep_weights(mlstm_w_in, mlstm_w_out, attn_w_in, attn_w_out, ffn_w_up, ffn_w_down)
    bias_tbl = _bias_table(rel_bias)
    run = functools.partial(_trunk, wts=wts, bias_tbl=bias_tbl, norm_g=norm_g,
                            mlstm_b_gate=mlstm_b_gate, mlstm_head_g=mlstm_head_g,
                            attn_sink=attn_sink, ffn_conv_w=ffn_conv_w, ffn_conv_b=ffn_conv_b,
                            final_g=final_g, tm=512)
    return run(x_prompt, mod[:, :nbp]), run(x_sample, mod[:, nbp:])
```

```python
import functools
import math

import jax
import jax.numpy as jnp
from jax import lax
from jax.experimental import pallas as pl
from jax.experimental.pallas import tpu as pltpu

F32 = jnp.float32
BF16 = jnp.bfloat16

D_MODEL = 1024
DEPTH = 2
M_HEADS = 4
M_DK = 256
M_DV = 256
A_HEADS = 16
A_KV = 4
A_GROUP = A_HEADS // A_KV
A_HD = 64
WINDOW = 128
A_BLOCK = 128
A_SPAN = A_BLOCK + 2 * WINDOW
NUM_BUCKETS = 32
MAX_DISTANCE = 128
D_FF = 2816
EPS = 1e-6

LANES = 128
MXU_DIM = 256
VMEM_LIMIT = 56 * 1024 * 1024

SCAN_CHUNK = 256
GATE_PAD = LANES
HALO = 16
FFN_CHUNK = MXU_DIM


def _cparams(n_axes):
    return pltpu.CompilerParams(dimension_semantics=("parallel",) * n_axes,
                                vmem_limit_bytes=VMEM_LIMIT)


def _resident(shape):
    nd = len(shape)
    return pl.BlockSpec(shape, lambda *_: (0,) * nd, pipeline_mode=pl.Buffered(1))


def _norm_mod(x, g, sc, sh):
    ms = jnp.mean(x * x, axis=-1, keepdims=True)
    y = x * lax.rsqrt(ms + EPS) * g
    return y * (1.0 + sc) + sh


def _adaln_kernel(c_ref, w_ref, b_ref, o_ref):
    c = c_ref[...]
    s = c * jax.nn.sigmoid(c)
    o_ref[0] = jnp.dot(s, w_ref[0], precision=lax.Precision.HIGHEST,
                       preferred_element_type=F32) + b_ref[0]


def _adaln(c_all, adaln_w, adaln_b):
    nb = c_all.shape[0]
    n = adaln_w.shape[-1]
    tn = 1536
    return pl.pallas_call(
        _adaln_kernel,
        grid=(DEPTH, n // tn),
        in_specs=[pl.BlockSpec((nb, D_MODEL), lambda l, j: (0, 0)),
                  pl.BlockSpec((1, D_MODEL, tn), lambda l, j: (l, 0, j)),
                  pl.BlockSpec((1, 1, tn), lambda l, j: (l, 0, j))],
        out_specs=pl.BlockSpec((1, nb, tn), lambda l, j: (l, 0, j)),
        out_shape=jax.ShapeDtypeStruct((DEPTH, nb, n), F32),
        compiler_params=_cparams(2),
        name="adaln",
    )(c_all, adaln_w, adaln_b.reshape(DEPTH, 1, n))


def _inproj_kernel(x_ref, g_ref, sc_ref, sh_ref, w_ref, o_ref, *, tn):
    h = _norm_mod(x_ref[0], g_ref[...], sc_ref[0, 0], sh_ref[0, 0]).astype(BF16)
    n = w_ref.shape[1]
    for j in range(n // tn):
        o_ref[0, :, j * tn:(j + 1) * tn] = jnp.dot(
            h, w_ref[:, j * tn:(j + 1) * tn], preferred_element_type=F32).astype(BF16)


def _inproj(x, g, mod4, sc_idx, sh_idx, w, *, tm):
    bsz, s, d = x.shape
    n = w.shape[1]
    return pl.pallas_call(
        functools.partial(_inproj_kernel, tn=512),
        grid=(bsz, s // tm),
        in_specs=[pl.BlockSpec((1, tm, d), lambda b, i: (b, i, 0)),
                  _resident((1, d)),
                  pl.BlockSpec((1, 1, 1, d), lambda b, i: (b, sc_idx, 0, 0)),
                  pl.BlockSpec((1, 1, 1, d), lambda b, i: (b, sh_idx, 0, 0)),
                  _resident((d, n))],
        out_specs=pl.BlockSpec((1, tm, n), lambda b, i: (b, i, 0)),
        out_shape=jax.ShapeDtypeStruct((bsz, s, n), BF16),
        compiler_params=_cparams(2),
        name="inproj",
    )(x, g.reshape(1, d), mod4, mod4, w)


def _inproj_m_kernel(x_ref, g_ref, sc_ref, sh_ref, wk_ref, wg_ref, wt_ref, k_ref, gate_ref, t_ref,
                     *, tn, chunk):
    h = _norm_mod(x_ref[0], g_ref[...], sc_ref[0, 0], sh_ref[0, 0]).astype(BF16)
    tm = h.shape[0]
    for j in range(wk_ref.shape[1] // tn):
        k_ref[0, :, j * tn:(j + 1) * tn] = jnp.dot(
            h, wk_ref[:, j * tn:(j + 1) * tn], preferred_element_type=F32).astype(BF16)
    gate_ref[0] = jnp.dot(h, wg_ref[...], preferred_element_type=F32)
    for j in range(wt_ref.shape[0] // tn):
        res = lax.dot_general(wt_ref[j * tn:(j + 1) * tn, :], h, (((1,), (1,)), ((), ())),
                              preferred_element_type=F32).astype(BF16)
        for cc in range(tm // chunk):
            t_ref[0, cc, j * tn:(j + 1) * tn, :] = res[:, cc * chunk:(cc + 1) * chunk]


def _inproj_m(x, g, mod4, w_k, w_gate, w_t, *, tm, chunk):
    bsz, s, d = x.shape
    nk = w_k.shape[1]
    nt = w_t.shape[0]
    return pl.pallas_call(
        functools.partial(_inproj_m_kernel, tn=512, chunk=chunk),
        grid=(bsz, s // tm),
        in_specs=[pl.BlockSpec((1, tm, d), lambda b, i: (b, i, 0)),
                  _resident((1, d)),
                  pl.BlockSpec((1, 1, 1, d), lambda b, i: (b, 1, 0, 0)),
                  pl.BlockSpec((1, 1, 1, d), lambda b, i: (b, 0, 0, 0)),
                  _resident((d, nk)), _resident((d, GATE_PAD)), _resident((nt, d))],
        out_specs=[pl.BlockSpec((1, tm, nk), lambda b, i: (b, i, 0)),
                   pl.BlockSpec((1, tm, GATE_PAD), lambda b, i: (b, i, 0)),
                   pl.BlockSpec((1, tm // chunk, nt, chunk), lambda b, i: (b, i, 0, 0))],
        out_shape=[jax.ShapeDtypeStruct((bsz, s, nk), BF16),
                   jax.ShapeDtypeStruct((bsz, s, GATE_PAD), F32),
                   jax.ShapeDtypeStruct((bsz, s // chunk, nt, chunk), BF16)],
        compiler_params=_cparams(2),
        name="inproj_m",
    )(x, g.reshape(1, d), mod4, mod4, w_k, w_gate, w_t)


N_PREP = 6


def _gateprep_kernel(g_ref, b_ref, p_ref, *, nc, chunk):
    s = nc * chunk
    gt = jnp.concatenate(
        [(g_ref[0, r * LANES:(r + 1) * LANES, :] + b_ref[...]).T[0:16] for r in range(s // LANES)],
        axis=1)
    ig = gt[0:8]
    fg = gt[8:16]
    row = lax.broadcasted_iota(jnp.int32, (8, s), 0)
    pos = lax.broadcasted_iota(jnp.int32, (8, s), 1) & (chunk - 1)
    fwd = row < M_HEADS

    def seg_scan(x, op, ident):
        k = 1
        while k < chunk:
            sh_f = jnp.where(pos >= k, pltpu.roll(x, k, axis=1), ident)
            sh_b = jnp.where(pos < chunk - k, pltpu.roll(x, s - k, axis=1), ident)
            x = op(x, jnp.where(fwd, sh_f, sh_b))
            k *= 2
        return x

    lf = jnp.minimum(fg, 0.0) - jnp.log1p(jnp.exp(-jnp.abs(fg)))
    bsum = seg_scan(lf, jnp.add, 0.0)
    beta = ig - bsum
    cmax = seg_scan(beta, jnp.maximum, -jnp.inf)

    fwd1 = fwd[:, :1]
    g_max = [jnp.max(beta[:, c * chunk:(c + 1) * chunk], axis=1, keepdims=True) for c in range(nc)]
    b_last = [jnp.where(fwd1, bsum[:, (c + 1) * chunk - 1:(c + 1) * chunk], bsum[:, c * chunk:c * chunk + 1])
              for c in range(nc)]
    m_f = [None] * nc
    m_b = [None] * nc
    mf = jnp.zeros((8, 1), F32)
    mb = jnp.zeros((8, 1), F32)
    for i in range(nc):
        j = nc - 1 - i
        m_f[i] = mf
        m_b[j] = mb
        mf = b_last[i] + jnp.maximum(mf, g_max[i])
        mb = b_last[j] + jnp.maximum(mb, g_max[j])
    wide = lambda v: jnp.broadcast_to(v, (8, chunk))
    m_in = jnp.concatenate([wide(jnp.where(fwd1, m_f[c], m_b[c])) for c in range(nc)], axis=1)
    g_all = jnp.concatenate([wide(g_max[c]) for c in range(nc)], axis=1)

    mx = jnp.maximum(cmax, m_in)
    m_c = jnp.maximum(m_in, g_all)
    quantities = (beta, mx, jnp.exp(m_in - mx), jnp.exp(-(bsum + mx)), jnp.exp(beta - m_c),
                  jnp.exp(m_in - m_c))
    row16 = lax.broadcasted_iota(jnp.int32, (16, s), 0)
    for h in range(M_HEADS):
        tile = jnp.zeros((16, s), F32)
        for d in range(2):
            for qi, val in enumerate(quantities):
                src = val[d * M_HEADS + h:d * M_HEADS + h + 1]
                tile = jnp.where(row16 == d * 8 + qi, jnp.broadcast_to(src, (16, s)), tile)
        for c in range(nc):
            p_ref[0, c, h] = tile[:, c * chunk:(c + 1) * chunk]


def _gateprep(gates, bias, chunk):
    bsz, s, _ = gates.shape
    nc = s // chunk
    return pl.pallas_call(
        functools.partial(_gateprep_kernel, nc=nc, chunk=chunk),
        grid=(bsz,),
        in_specs=[pl.BlockSpec((1, s, GATE_PAD), lambda b: (b, 0, 0)),
                  _resident((1, GATE_PAD))],
        out_specs=pl.BlockSpec((1, nc, M_HEADS, 16, chunk), lambda b: (b, 0, 0, 0, 0)),
        out_shape=jax.ShapeDtypeStruct((bsz, nc, M_HEADS, 16, chunk), F32),
        compiler_params=_cparams(1),
        name="gateprep",
    )(gates, bias)


def _mlstm_kernel(qt_ref, vt_ref, ot_ref, k_ref, p_ref, hg_ref, at_ref, acc, ct_scr, n_scr,
                  *, nc, chunk):
    L = chunk
    acc[...] = jnp.zeros_like(acc)
    ct_scr[...] = jnp.zeros_like(ct_scr)
    n_scr[...] = jnp.zeros_like(n_scr)
    s_idx = lax.broadcasted_iota(jnp.int32, (L, L), 0)
    t_idx = lax.broadcasted_iota(jnp.int32, (L, L), 1)
    tri = (s_idx <= t_idx, s_idx >= t_idx)
    n_pad = 16

    def step(c, d):
        qt = qt_ref[0, c]
        vt = vt_ref[0, c]
        k = k_ref[0, pl.ds(pl.multiple_of(c * L, L), L), :]
        t = p_ref[0, c, 0, d * 8:(d + 1) * 8, :]
        beta, mx, w_inter, neg_mt, w, decay = (t[i:i + 1] for i in range(N_PREP))
        ct_old = ct_scr[d]
        n_old = n_scr[d]

        lhs = jnp.concatenate([k, ct_old.astype(BF16),
                               jnp.concatenate([n_old, n_old], axis=0).astype(BF16)], axis=0)
        r1 = jnp.dot(lhs, qt, preferred_element_type=F32)
        d_t = jnp.where(tri[d], jnp.exp(jnp.broadcast_to(beta, (L, L)).T - mx), 0.0)
        p_t = r1[0:L] * d_t
        inter = r1[L:L + M_DV]
        qn = r1[L + M_DV:L + M_DV + 1]
        num = jnp.dot(vt, p_t.astype(BF16), preferred_element_type=F32) + w_inter * inter
        den = jnp.sum(p_t, axis=0, keepdims=True) + w_inter * qn
        acc[c] += num * (1.0 / jnp.maximum(jnp.abs(den), neg_mt))

        vts = (vt.astype(F32) * w).astype(BF16)
        lhs2 = jnp.concatenate([vts, jnp.broadcast_to(w, (n_pad, L)).astype(BF16)], axis=0)
        r2 = jnp.dot(lhs2, k, preferred_element_type=F32)
        ct_scr[d] = decay * ct_old + r2[0:M_DV]
        n_scr[d] = decay * n_old + r2[M_DV:M_DV + 8]

    def body(i, carry):
        step(i, 0)
        step(nc - 1 - i, 1)
        return carry

    lax.fori_loop(0, nc, body, 0)

    def fin(c, carry):
        hs = acc[c]
        ms = jnp.mean(hs * hs, axis=0, keepdims=True)
        y = hs * lax.rsqrt(ms + EPS) * hg_ref[0]
        gate = jax.nn.sigmoid(ot_ref[0, c].astype(F32))
        at_ref[0, c] = (y * gate).astype(BF16)
        return carry

    lax.fori_loop(0, nc, fin, 0)


def _mlstm_scan(k_tok, qvo_t, prep, head_g, chunk):
    bsz, s, _ = k_tok.shape
    nc = s // chunk
    blk_t = lambda off: pl.BlockSpec((1, nc, M_DK, chunk), lambda b, h: (b, 0, off + h, 0))
    return pl.pallas_call(
        functools.partial(_mlstm_kernel, nc=nc, chunk=chunk),
        grid=(bsz, M_HEADS),
        in_specs=[blk_t(0), blk_t(M_HEADS), blk_t(2 * M_HEADS),
                  pl.BlockSpec((1, s, M_DK), lambda b, h: (b, 0, h)),
                  pl.BlockSpec((1, nc, 1, 16, chunk), lambda b, h: (b, 0, h, 0, 0)),
                  pl.BlockSpec((1, M_DV, 1), lambda b, h: (h, 0, 0))],
        out_specs=pl.BlockSpec((1, nc, M_DV, chunk), lambda b, h: (b, 0, h, 0)),
        out_shape=jax.ShapeDtypeStruct((bsz, nc, M_HEADS * M_DV, chunk), BF16),
        scratch_shapes=[pltpu.VMEM((nc, M_DV, chunk), F32),
                        pltpu.VMEM((2, M_DV, M_DK), F32),
                        pltpu.VMEM((2, 8, M_DK), F32)],
        compiler_params=_cparams(2),
        name="mlstm_scan",
    )(qvo_t, qvo_t, qvo_t, k_tok, prep, head_g.reshape(M_HEADS, M_DV, 1))


def _outproj_kernel(a_ref, w_ref, x_ref, g_ref, o_ref):
    y = jnp.dot(a_ref[0], w_ref[...], preferred_element_type=F32)
    o_ref[0] = x_ref[0] + g_ref[0, 0] * y


def _outproj_t_kernel(a_ref, w_ref, x_ref, g_ref, o_ref, *, chunk):
    for cc in range(a_ref.shape[1]):
        y = lax.dot_general(a_ref[0, cc], w_ref[...], (((0,), (0,)), ((), ())),
                            preferred_element_type=F32)
        rows = slice(cc * chunk, (cc + 1) * chunk)
        o_ref[0, rows, :] = x_ref[0, rows, :] + g_ref[0, 0] * y


def _outproj(a, w, x, mod4, g_idx, *, tm, chunk=None):
    bsz, s, d = x.shape
    kdim = w.shape[0]
    if chunk is None:
        body = _outproj_kernel
        a_spec = pl.BlockSpec((1, tm, kdim), lambda b, i: (b, i, 0))
    else:
        body = functools.partial(_outproj_t_kernel, chunk=chunk)
        a_spec = pl.BlockSpec((1, tm // chunk, kdim, chunk), lambda b, i: (b, i, 0, 0))
    return pl.pallas_call(
        body,
        grid=(bsz, s // tm),
        in_specs=[a_spec,
                  _resident((kdim, d)),
                  pl.BlockSpec((1, tm, d), lambda b, i: (b, i, 0)),
                  pl.BlockSpec((1, 1, 1, d), lambda b, i: (b, g_idx, 0, 0))],
        out_specs=pl.BlockSpec((1, tm, d), lambda b, i: (b, i, 0)),
        out_shape=jax.ShapeDtypeStruct((bsz, s, d), F32),
        compiler_params=_cparams(2),
        name="outproj" if chunk is None else "outproj_t",
    )(a, w, x, mod4)


def _ffn_kernel(x_ref, xp_ref, xn_ref, g_ref, sc_ref, sh_ref, gate_ref, wup_ref, cw_ref, cb_ref,
                wdn_ref, fg_ref, o_ref, h_scr, a_scr, *, tm, final):
    i = pl.program_id(1)
    last = pl.num_programs(1) - 1
    g = g_ref[...]
    sc = sc_ref[0, 0]
    sh = sh_ref[0, 0]
    x = x_ref[0]
    hp = jnp.where(i > 0, _norm_mod(xp_ref[0], g, sc, sh), 0.0)
    hn = jnp.where(i < last, _norm_mod(xn_ref[0], g, sc, sh), 0.0)
    h_scr[0:HALO] = hp.astype(BF16)
    h_scr[HALO:HALO + tm] = _norm_mod(x, g, sc, sh).astype(BF16)
    h_scr[HALO + tm:] = hn.astype(BF16)
    h = h_scr[...]
    ext = tm + 2 * HALO

    def conv(u, off):
        w0 = cw_ref[0:1, off:off + FFN_CHUNK]
        w1 = cw_ref[1:2, off:off + FFN_CHUNK]
        w2 = cw_ref[2:3, off:off + FFN_CHUNK]
        prev = pltpu.roll(u, 1, axis=0)
        nxt = pltpu.roll(u, ext - 1, axis=0)
        out = w0 * prev + w1 * u + w2 * nxt + cb_ref[:, off:off + FFN_CHUNK]
        return out[HALO:HALO + tm]

    for f in range(D_FF // FFN_CHUNK):
        og = f * FFN_CHUNK
        ov = D_FF + f * FFN_CHUNK
        ug = conv(jnp.dot(h, wup_ref[:, og:og + FFN_CHUNK], preferred_element_type=F32), og)
        uv = conv(jnp.dot(h, wup_ref[:, ov:ov + FFN_CHUNK], preferred_element_type=F32), ov)
        a_scr[:, og:og + FFN_CHUNK] = (ug * jax.nn.sigmoid(ug) * uv).astype(BF16)

    y = jnp.dot(a_scr[...], wdn_ref[...], preferred_element_type=F32)
    out = x + gate_ref[0, 0] * y
    if final:
        ms = jnp.mean(out * out, axis=-1, keepdims=True)
        out = out * lax.rsqrt(ms + EPS) * fg_ref[...]
    o_ref[0] = out


def _ffn(x, g, mod4, w_up, conv_w, conv_b, w_down, final_g, *, tm, final):
    bsz, s, d = x.shape
    hb = tm // HALO
    nh = s // HALO
    return pl.pallas_call(
        functools.partial(_ffn_kernel, tm=tm, final=final),
        grid=(bsz, s // tm),
        in_specs=[pl.BlockSpec((1, tm, d), lambda b, i: (b, i, 0)),
                  pl.BlockSpec((1, HALO, d), lambda b, i: (b, jnp.maximum(i * hb - 1, 0), 0)),
                  pl.BlockSpec((1, HALO, d), lambda b, i: (b, jnp.minimum((i + 1) * hb, nh - 1), 0)),
                  _resident((1, d)),
                  pl.BlockSpec((1, 1, 1, d), lambda b, i: (b, 4, 0, 0)),
                  pl.BlockSpec((1, 1, 1, d), lambda b, i: (b, 3, 0, 0)),
                  pl.BlockSpec((1, 1, 1, d), lambda b, i: (b, 5, 0, 0)),
                  _resident((d, 2 * D_FF)),
                  _resident((3, 2 * D_FF)),
                  _resident((1, 2 * D_FF)),
                  _resident((D_FF, d)),
                  _resident((1, d))],
        out_specs=pl.BlockSpec((1, tm, d), lambda b, i: (b, i, 0)),
        out_shape=jax.ShapeDtypeStruct((bsz, s, d), F32),
        scratch_shapes=[pltpu.VMEM((tm + 2 * HALO, d), BF16),
                        pltpu.VMEM((tm, D_FF), BF16)],
        compiler_params=_cparams(2),
        name="ffn_final" if final else "ffn",
    )(x, x, x, g.reshape(1, d), mod4, mod4, mod4, w_up, conv_w, conv_b.reshape(1, -1), w_down,
      final_g.reshape(1, d))


def _t5_bucket(rel):
    nb = NUM_BUCKETS // 2
    max_exact = nb // 2
    ret = jnp.where(rel > 0, nb, 0)
    n = jnp.abs(rel)
    nf = jnp.maximum(n, 1).astype(jnp.float32)
    large = max_exact + (jnp.log(nf / max_exact) / math.log(MAX_DISTANCE / max_exact)
                         * (nb - max_exact)).astype(jnp.int32)
    large = jnp.minimum(large, nb - 1)
    return ret + jnp.where(n < max_exact, n, large)


def _bias_kernel(bucket_ref, rb_ref, o_ref):
    hh = pl.program_id(0)
    head = (hh % A_KV) * A_GROUP + hh // A_KV
    bucket = bucket_ref[...]
    q = lax.broadcasted_iota(jnp.int32, bucket.shape, 0)
    c = lax.broadcasted_iota(jnp.int32, bucket.shape, 1)
    band = jnp.abs(c - WINDOW - q) <= WINDOW
    val = jnp.zeros(bucket.shape, F32)
    for kb in range(NUM_BUCKETS):
        val = jnp.where(bucket == kb, rb_ref[kb, head], val)
    o_ref[0] = jnp.where(band, val, -jnp.inf)


def _bias_table(rel_bias):
    a_idx = jnp.arange(A_BLOCK)[:, None]
    c_idx = jnp.arange(A_SPAN)[None, :]
    bucket = _t5_bucket(c_idx - WINDOW - a_idx).astype(jnp.int32)
    return pl.pallas_call(
        _bias_kernel,
        grid=(A_HEADS,),
        in_specs=[pl.BlockSpec((A_BLOCK, A_SPAN), lambda h: (0, 0)),
                  pl.BlockSpec(memory_space=pltpu.SMEM)],
        out_specs=pl.BlockSpec((1, A_BLOCK, A_SPAN), lambda h: (h, 0, 0)),
        out_shape=jax.ShapeDtypeStruct((A_HEADS, A_BLOCK, A_SPAN), F32),
        compiler_params=_cparams(1),
        name="bias_table",
    )(bucket, rel_bias)


def _swa_kernel(q_ref, kp_ref, kc_ref, kn_ref, vp_ref, vc_ref, vn_ref, bias_ref, sink_ref, o_ref):
    i = pl.program_id(1)
    last = pl.num_programs(1) - 1
    kcat = jnp.concatenate([kp_ref[0], kc_ref[0], kn_ref[0]], axis=0)
    vcat = jnp.concatenate([vp_ref[0], vc_ref[0], vn_ref[0]], axis=0)
    colk = lax.broadcasted_iota(jnp.int32, (1, A_SPAN), 1)
    edge = ((colk < WINDOW) & (i == 0)) | ((colk >= WINDOW + A_BLOCK) & (i == last))
    lane = lax.broadcasted_iota(jnp.int32, (A_BLOCK, A_KV * A_HD), 1)
    for g in range(A_GROUP):
        qg = q_ref[0, :, g * MXU_DIM:(g + 1) * MXU_DIM]
        qm = jnp.concatenate(
            [jnp.where(lane // A_HD == kv, qg, jnp.zeros_like(qg)) for kv in range(A_KV)], axis=0)
        s_all = lax.dot_general(qm, kcat, (((1,), (1,)), ((), ())), preferred_element_type=F32)
        ps = []
        rden = []
        for kv in range(A_KV):
            s = s_all[kv * A_BLOCK:(kv + 1) * A_BLOCK] + bias_ref[g * A_KV + kv]
            s = jnp.where(edge, -jnp.inf, s)
            sink = sink_ref[kv * A_GROUP + g]
            m = jnp.maximum(jnp.max(s, axis=-1, keepdims=True), sink)
            p = jnp.exp(s - m)
            den = jnp.sum(p, axis=-1, keepdims=True) + jnp.exp(sink - m)
            ps.append(p.astype(BF16))
            rden.append(1.0 / den)
        pv = jnp.dot(jnp.concatenate(ps, axis=0), vcat, preferred_element_type=F32)
        out = jnp.zeros((A_BLOCK, A_KV * A_HD), F32)
        for kv in range(A_KV):
            out = jnp.where(lane // A_HD == kv, pv[kv * A_BLOCK:(kv + 1) * A_BLOCK] * rden[kv], out)
        o_ref[0, :, g * MXU_DIM:(g + 1) * MXU_DIM] = out.astype(BF16)


def _swa(qkv, bias_tbl, sink):
    bsz, s, _ = qkv.shape
    nb = s // A_BLOCK
    kcol = A_HEADS * A_HD // MXU_DIM
    vcol = kcol + 1
    prev = lambda b, i: jnp.maximum(i - 1, 0)
    nxt = lambda b, i: jnp.minimum(i + 1, nb - 1)
    kv_spec = lambda col, row: pl.BlockSpec((1, A_BLOCK, MXU_DIM), lambda b, i: (b, row(b, i), col))
    cur = lambda b, i: i
    return pl.pallas_call(
        _swa_kernel,
        grid=(bsz, nb),
        in_specs=[pl.BlockSpec((1, A_BLOCK, A_HEADS * A_HD), lambda b, i: (b, i, 0)),
                  kv_spec(kcol, prev), kv_spec(kcol, cur), kv_spec(kcol, nxt),
                  kv_spec(vcol, prev), kv_spec(vcol, cur), kv_spec(vcol, nxt),
                  _resident((A_HEADS, A_BLOCK, A_SPAN)),
                  pl.BlockSpec(memory_space=pltpu.SMEM)],
        out_specs=pl.BlockSpec((1, A_BLOCK, A_HEADS * A_HD), lambda b, i: (b, i, 0)),
        out_shape=jax.ShapeDtypeStruct((bsz, s, A_HEADS * A_HD), BF16),
        compiler_params=_cparams(2),
        name="swa",
    )(qkv, qkv, qkv, qkv, qkv, qkv, qkv, bias_tbl, sink)


def _prep_weights(mlstm_w_in, mlstm_b_gate, mlstm_w_out, attn_w_in, attn_w_out, ffn_w_up, ffn_w_down):
    qk = M_HEADS * M_DK
    w_m = mlstm_w_in[0]
    w_q = w_m[:, :qk] * (M_DK ** -0.5)
    w_t = jnp.concatenate([w_q, w_m[:, 2 * qk:4 * qk]], axis=1).T.astype(BF16)
    w_k = w_m[:, qk:2 * qk].astype(BF16)
    perm = lambda g: g.reshape(-1, 2, 2, M_HEADS).swapaxes(1, 2).reshape(-1, 4 * M_HEADS)
    w_gate = jnp.zeros((D_MODEL, GATE_PAD), F32).at[:, :4 * M_HEADS].set(perm(w_m[:, 4 * qk:]))
    b_gate = jnp.zeros((1, GATE_PAD), F32).at[:, :4 * M_HEADS].set(perm(mlstm_b_gate[0][None]))
    qw = A_HEADS * A_HD
    w_a = attn_w_in[0]
    wq = w_a[:, :qw].reshape(D_MODEL, A_KV, A_GROUP, A_HD).transpose(0, 2, 1, 3).reshape(D_MODEL, qw)
    w_attn = jnp.concatenate([wq * (A_HD ** -0.5), w_a[:, qw:]], axis=1).astype(BF16)
    wo_attn = attn_w_out[0].reshape(A_KV, A_GROUP, A_HD, D_MODEL).transpose(1, 0, 2, 3)
    wo_attn = wo_attn.reshape(qw, D_MODEL).astype(BF16)
    return dict(w_k=w_k, w_t=w_t, w_gate=w_gate.astype(BF16), b_gate=b_gate,
                wo_m=mlstm_w_out[0].astype(BF16), w_attn=w_attn, wo_attn=wo_attn,
                w_up=ffn_w_up.astype(BF16), w_down=ffn_w_down.astype(BF16))


def _trunk(x, mod, wts, bias_tbl, norm_g, mlstm_head_g, attn_sink, ffn_conv_w, ffn_conv_b, final_g,
           *, tm):
    bsz, s, d = x.shape
    mod4 = mod[0].reshape(bsz, 6, 1, d)
    k_tok, gates, qvo_t = _inproj_m(x, norm_g[0, 0], mod4, wts["w_k"], wts["w_gate"], wts["w_t"],
                                    tm=tm, chunk=SCAN_CHUNK)
    prep = _gateprep(gates, wts["b_gate"], SCAN_CHUNK)
    a_t = _mlstm_scan(k_tok, qvo_t, prep, mlstm_head_g[0], SCAN_CHUNK)
    x = _outproj(a_t, wts["wo_m"], x, mod4, 2, tm=tm, chunk=SCAN_CHUNK)
    x = _ffn(x, norm_g[0, 1], mod4, wts["w_up"][0], ffn_conv_w[0], ffn_conv_b[0], wts["w_down"][0],
             final_g, tm=tm, final=False)
    mod4 = mod[1].reshape(bsz, 6, 1, d)
    qkv = _inproj(x, norm_g[1, 0], mod4, 1, 0, wts["w_attn"], tm=tm)
    a = _swa(qkv, bias_tbl, attn_sink[0])
    x = _outproj(a, wts["wo_attn"], x, mod4, 2, tm=tm)
    return _ffn(x, norm_g[1, 1], mod4, wts["w_up"][1], ffn_conv_w[1], ffn_conv_b[1], wts["w_down"][1],
                final_g, tm=tm, final=True)


def kernel(x_prompt, x_sample, c_prompt, c_sample, adaln_w, adaln_b, norm_g, mlstm_w_in, mlstm_b_gate, mlstm_head_g, mlstm_w_out, attn_w_in, attn_sink, attn_w_out, rel_bias, ffn_w_up, ffn_conv_w, ffn_conv_b, ffn_w_down, final_g):
    nbp = x_prompt.shape[0]
    mod = _adaln(jnp.concatenate([c_prompt, c_sample], axis=0), adaln_w, adaln_b)
    wts = _prep_weights(mlstm_w_in, mlstm_b_gate, mlstm_w_out, attn_w_in, attn_w_out, ffn_w_up, ffn_w_down)
    bias_tbl = _bias_table(rel_bias)
    run = functools.partial(_trunk, wts=wts, bias_tbl=bias_tbl, norm_g=norm_g,
                            mlstm_head_g=mlstm_head_g, attn_sink=attn_sink, ffn_conv_w=ffn_conv_w,
                            ffn_conv_b=ffn_conv_b, final_g=final_g, tm=512)
    return run(x_prompt, mod[:, :nbp]), run(x_sample, mod[:, nbp:])
```

```python
import functools
import math

import jax
import jax.numpy as jnp
from jax import lax
from jax.experimental import pallas as pl
from jax.experimental.pallas import tpu as pltpu

F32 = jnp.float32
BF16 = jnp.bfloat16

D_MODEL = 1024
DEPTH = 2
M_HEADS = 4
M_DK = 256
M_DV = 256
A_HEADS = 16
A_KV = 4
A_GROUP = A_HEADS // A_KV
A_HD = 64
WINDOW = 128
A_BLOCK = 128
A_SPAN = A_BLOCK + 2 * WINDOW
NUM_BUCKETS = 32
MAX_DISTANCE = 128
D_FF = 2816
EPS = 1e-6
LOG2E = 1.4426950408889634

LANES = 128
MXU_DIM = 256
VMEM_LIMIT = 56 * 1024 * 1024

SCAN_CHUNK = 256
GATE_PAD = LANES
HALO = 16
FFN_CHUNK = MXU_DIM


def _cparams(n_axes):
    return pltpu.CompilerParams(dimension_semantics=("parallel",) * n_axes,
                                vmem_limit_bytes=VMEM_LIMIT)


def _resident(shape):
    nd = len(shape)
    return pl.BlockSpec(shape, lambda *_: (0,) * nd, pipeline_mode=pl.Buffered(1))


def _norm_mod(x, g, sc, sh):
    ms = jnp.mean(x * x, axis=-1, keepdims=True)
    y = x * lax.rsqrt(ms + EPS) * g
    return y * (1.0 + sc) + sh


def _adaln_kernel(c_ref, w_ref, b_ref, o_ref):
    c = c_ref[...]
    s = c * jax.nn.sigmoid(c)
    o_ref[0] = jnp.dot(s, w_ref[0], precision=lax.Precision.HIGHEST,
                       preferred_element_type=F32) + b_ref[0]


def _adaln(c_all, adaln_w, adaln_b):
    nb = c_all.shape[0]
    n = adaln_w.shape[-1]
    tn = 1536
    return pl.pallas_call(
        _adaln_kernel,
        grid=(DEPTH, n // tn),
        in_specs=[pl.BlockSpec((nb, D_MODEL), lambda l, j: (0, 0)),
                  pl.BlockSpec((1, D_MODEL, tn), lambda l, j: (l, 0, j)),
                  pl.BlockSpec((1, 1, tn), lambda l, j: (l, 0, j))],
        out_specs=pl.BlockSpec((1, nb, tn), lambda l, j: (l, 0, j)),
        out_shape=jax.ShapeDtypeStruct((DEPTH, nb, n), F32),
        compiler_params=_cparams(2),
        name="adaln",
    )(c_all, adaln_w, adaln_b.reshape(DEPTH, 1, n))


def _inproj_split_kernel(x_ref, g_ref, sc_ref, sh_ref, wk_ref, wt_ref, *rest, tn, chunk, with_gates):
    if with_gates:
        wg_ref, k_ref, t_ref, gate_ref = rest
    else:
        k_ref, t_ref = rest
    h = _norm_mod(x_ref[0], g_ref[...], sc_ref[0, 0], sh_ref[0, 0]).astype(BF16)
    tm = h.shape[0]
    nk = wk_ref.shape[1]
    tk = min(tn, nk)
    for j in range(nk // tk):
        k_ref[0, :, j * tk:(j + 1) * tk] = jnp.dot(
            h, wk_ref[:, j * tk:(j + 1) * tk], preferred_element_type=F32).astype(BF16)
    if with_gates:
        gate_ref[0] = jnp.dot(h, wg_ref[...], preferred_element_type=F32)
    for j in range(wt_ref.shape[0] // tn):
        res = lax.dot_general(wt_ref[j * tn:(j + 1) * tn, :], h, (((1,), (1,)), ((), ())),
                              preferred_element_type=F32).astype(BF16)
        for cc in range(tm // chunk):
            t_ref[0, cc, j * tn:(j + 1) * tn, :] = res[:, cc * chunk:(cc + 1) * chunk]


def _inproj_split(x, g, mod4, w_k, w_t, w_gate=None, *, tm, chunk, tn):
    bsz, s, d = x.shape
    nk = w_k.shape[1]
    nt = w_t.shape[0]
    with_gates = w_gate is not None
    in_specs = [pl.BlockSpec((1, tm, d), lambda b, i: (b, i, 0)),
                _resident((1, d)),
                pl.BlockSpec((1, 1, 1, d), lambda b, i: (b, 1, 0, 0)),
                pl.BlockSpec((1, 1, 1, d), lambda b, i: (b, 0, 0, 0)),
                _resident((d, nk)), _resident((nt, d))]
    args = [x, g.reshape(1, d), mod4, mod4, w_k, w_t]
    out_specs = [pl.BlockSpec((1, tm, nk), lambda b, i: (b, i, 0)),
                 pl.BlockSpec((1, tm // chunk, nt, chunk), lambda b, i: (b, i, 0, 0))]
    out_shape = [jax.ShapeDtypeStruct((bsz, s, nk), BF16),
                 jax.ShapeDtypeStruct((bsz, s // chunk, nt, chunk), BF16)]
    if with_gates:
        in_specs.append(_resident((d, GATE_PAD)))
        args.append(w_gate)
        out_specs.append(pl.BlockSpec((1, tm, GATE_PAD), lambda b, i: (b, i, 0)))
        out_shape.append(jax.ShapeDtypeStruct((bsz, s, GATE_PAD), F32))
    return pl.pallas_call(
        functools.partial(_inproj_split_kernel, tn=tn, chunk=chunk, with_gates=with_gates),
        grid=(bsz, s // tm),
        in_specs=in_specs, out_specs=out_specs, out_shape=out_shape,
        compiler_params=_cparams(2),
        name="inproj_m" if with_gates else "inproj_a",
    )(*args)


N_PREP = 6


def _gateprep_kernel(g_ref, b_ref, p_ref, *, nc, chunk):
    s = nc * chunk
    gt = jnp.concatenate(
        [(g_ref[0, r * LANES:(r + 1) * LANES, :] + b_ref[...]).T[0:16] for r in range(s // LANES)],
        axis=1)
    ig = gt[0:8]
    fg = gt[8:16]
    row = lax.broadcasted_iota(jnp.int32, (8, s), 0)
    pos = lax.broadcasted_iota(jnp.int32, (8, s), 1) & (chunk - 1)
    fwd = row < M_HEADS

    def seg_scan(x, op, ident):
        k = 1
        while k < chunk:
            sh_f = jnp.where(pos >= k, pltpu.roll(x, k, axis=1), ident)
            sh_b = jnp.where(pos < chunk - k, pltpu.roll(x, s - k, axis=1), ident)
            x = op(x, jnp.where(fwd, sh_f, sh_b))
            k *= 2
        return x

    lf = jnp.minimum(fg, 0.0) - jnp.log1p(jnp.exp(-jnp.abs(fg)))
    bsum = seg_scan(lf, jnp.add, 0.0)
    beta = ig - bsum
    cmax = seg_scan(beta, jnp.maximum, -jnp.inf)

    fwd1 = fwd[:, :1]
    g_max = [jnp.max(beta[:, c * chunk:(c + 1) * chunk], axis=1, keepdims=True) for c in range(nc)]
    b_last = [jnp.where(fwd1, bsum[:, (c + 1) * chunk - 1:(c + 1) * chunk], bsum[:, c * chunk:c * chunk + 1])
              for c in range(nc)]
    m_f = [None] * nc
    m_b = [None] * nc
    mf = jnp.zeros((8, 1), F32)
    mb = jnp.zeros((8, 1), F32)
    for i in range(nc):
        j = nc - 1 - i
        m_f[i] = mf
        m_b[j] = mb
        mf = b_last[i] + jnp.maximum(mf, g_max[i])
        mb = b_last[j] + jnp.maximum(mb, g_max[j])
    wide = lambda v: jnp.broadcast_to(v, (8, chunk))
    m_in = jnp.concatenate([wide(jnp.where(fwd1, m_f[c], m_b[c])) for c in range(nc)], axis=1)
    g_all = jnp.concatenate([wide(g_max[c]) for c in range(nc)], axis=1)

    mx = jnp.maximum(cmax, m_in)
    m_c = jnp.maximum(m_in, g_all)
    quantities = (beta, mx, jnp.exp(m_in - mx), jnp.exp(-(bsum + mx)), jnp.exp(beta - m_c),
                  jnp.exp(m_in - m_c))
    row16 = lax.broadcasted_iota(jnp.int32, (16, s), 0)
    for h in range(M_HEADS):
        tile = jnp.zeros((16, s), F32)
        for d in range(2):
            for qi, val in enumerate(quantities):
                src = val[d * M_HEADS + h:d * M_HEADS + h + 1]
                tile = jnp.where(row16 == d * 8 + qi, jnp.broadcast_to(src, (16, s)), tile)
        for c in range(nc):
            p_ref[0, c, h] = tile[:, c * chunk:(c + 1) * chunk]


def _gateprep(gates, bias, chunk):
    bsz, s, _ = gates.shape
    nc = s // chunk
    return pl.pallas_call(
        functools.partial(_gateprep_kernel, nc=nc, chunk=chunk),
        grid=(bsz,),
        in_specs=[pl.BlockSpec((1, s, GATE_PAD), lambda b: (b, 0, 0)),
                  _resident((1, GATE_PAD))],
        out_specs=pl.BlockSpec((1, nc, M_HEADS, 16, chunk), lambda b: (b, 0, 0, 0, 0)),
        out_shape=jax.ShapeDtypeStruct((bsz, nc, M_HEADS, 16, chunk), F32),
        compiler_params=_cparams(1),
        name="gateprep",
    )(gates, bias)


def _mlstm_kernel(qt_ref, vt_ref, ot_ref, k_ref, p_ref, hg_ref, at_ref, acc, ct_scr, n_scr,
                  *, nc, chunk):
    L = chunk
    ct_scr[...] = jnp.zeros_like(ct_scr)
    n_scr[...] = jnp.zeros_like(n_scr)
    s_idx = lax.broadcasted_iota(jnp.int32, (L, L), 0)
    t_idx = lax.broadcasted_iota(jnp.int32, (L, L), 1)
    tri = (s_idx <= t_idx, s_idx >= t_idx)
    n_pad = 16

    def body(i, carry, *, finish):
        cs = (i, nc - 1 - i)
        qt = [qt_ref[0, c] for c in cs]
        vt = [vt_ref[0, c] for c in cs]
        k = [k_ref[0, pl.ds(pl.multiple_of(c * L, L), L), :] for c in cs]
        prm = [p_ref[0, cs[d], 0, d * 8:(d + 1) * 8, :] for d in range(2)]
        ct_old = [ct_scr[d] for d in range(2)]
        n_old = [n_scr[d] for d in range(2)]
        r1 = []
        for d in range(2):
            lhs = jnp.concatenate([k[d], ct_old[d].astype(BF16),
                                   jnp.concatenate([n_old[d], n_old[d]], axis=0).astype(BF16)], axis=0)
            r1.append(jnp.dot(lhs, qt[d], preferred_element_type=F32))
        r2 = []
        for d in range(2):
            w = prm[d][4:5]
            vts = (vt[d].astype(F32) * w).astype(BF16)
            lhs2 = jnp.concatenate([vts, jnp.broadcast_to(w, (n_pad, L)).astype(BF16)], axis=0)
            r2.append(jnp.dot(lhs2, k[d], preferred_element_type=F32))
        for d in range(2):
            beta, mx, w_inter, neg_mt, _, decay = (prm[d][j:j + 1] for j in range(N_PREP))
            d_t = jnp.where(tri[d], jnp.exp(jnp.broadcast_to(beta, (L, L)).T - mx), 0.0)
            p_t = r1[d][0:L] * d_t
            inter = r1[d][L:L + M_DV]
            qn = r1[d][L + M_DV:L + M_DV + 1]
            num = jnp.dot(vt[d], p_t.astype(BF16), preferred_element_type=F32) + w_inter * inter
            den = jnp.sum(p_t, axis=0, keepdims=True) + w_inter * qn
            hout = num * (1.0 / jnp.maximum(jnp.abs(den), neg_mt))
            if finish:
                hs = acc[cs[d]] + hout
                ms = jnp.mean(hs * hs, axis=0, keepdims=True)
                y = hs * lax.rsqrt(ms + EPS) * hg_ref[0]
                gate = jax.nn.sigmoid(ot_ref[0, cs[d]].astype(F32))
                at_ref[0, cs[d]] = (y * gate).astype(BF16)
            else:
                acc[cs[d]] = hout
            ct_scr[d] = decay * ct_old[d] + r2[d][0:M_DV]
            n_scr[d] = decay * n_old[d] + r2[d][M_DV:M_DV + 8]
        return carry

    half = nc // 2
    lax.fori_loop(0, half, functools.partial(body, finish=False), 0)
    lax.fori_loop(half, nc, functools.partial(body, finish=True), 0)


def _mlstm_scan(k_tok, qvo_t, prep, head_g, chunk):
    bsz, s, _ = k_tok.shape
    nc = s // chunk
    assert nc % 2 == 0
    blk_t = lambda off: pl.BlockSpec((1, nc, M_DK, chunk), lambda b, h: (b, 0, off + h, 0))
    return pl.pallas_call(
        functools.partial(_mlstm_kernel, nc=nc, chunk=chunk),
        grid=(bsz, M_HEADS),
        in_specs=[blk_t(0), blk_t(M_HEADS), blk_t(2 * M_HEADS),
                  pl.BlockSpec((1, s, M_DK), lambda b, h: (b, 0, h)),
                  pl.BlockSpec((1, nc, 1, 16, chunk), lambda b, h: (b, 0, h, 0, 0)),
                  pl.BlockSpec((1, M_DV, 1), lambda b, h: (h, 0, 0))],
        out_specs=pl.BlockSpec((1, nc, M_DV, chunk), lambda b, h: (b, 0, h, 0)),
        out_shape=jax.ShapeDtypeStruct((bsz, nc, M_HEADS * M_DV, chunk), BF16),
        scratch_shapes=[pltpu.VMEM((nc, M_DV, chunk), F32),
                        pltpu.VMEM((2, M_DV, M_DK), F32),
                        pltpu.VMEM((2, 8, M_DK), F32)],
        compiler_params=_cparams(2),
        name="mlstm_scan",
    )(qvo_t, qvo_t, qvo_t, k_tok, prep, head_g.reshape(M_HEADS, M_DV, 1))


def _outproj_t_kernel(a_ref, w_ref, x_ref, g_ref, o_ref):
    a = jnp.concatenate([a_ref[0, cc] for cc in range(a_ref.shape[1])], axis=1)
    y = lax.dot_general(a, w_ref[...], (((0,), (0,)), ((), ())), preferred_element_type=F32)
    o_ref[0] = x_ref[0] + g_ref[0, 0] * y


def _outproj(a_t, w, x, mod4, g_idx, *, tm, chunk):
    bsz, s, d = x.shape
    kdim = w.shape[0]
    return pl.pallas_call(
        _outproj_t_kernel,
        grid=(bsz, s // tm),
        in_specs=[pl.BlockSpec((1, tm // chunk, kdim, chunk), lambda b, i: (b, i, 0, 0)),
                  _resident((kdim, d)),
                  pl.BlockSpec((1, tm, d), lambda b, i: (b, i, 0)),
                  pl.BlockSpec((1, 1, 1, d), lambda b, i: (b, g_idx, 0, 0))],
        out_specs=pl.BlockSpec((1, tm, d), lambda b, i: (b, i, 0)),
        out_shape=jax.ShapeDtypeStruct((bsz, s, d), F32),
        compiler_params=_cparams(2),
        name="outproj_t",
    )(a_t, w, x, mod4)


def _ffn_kernel(x_ref, xp_ref, xn_ref, g_ref, sc_ref, sh_ref, gate_ref, wup_ref, cw_ref, cb_ref,
                wdn_ref, fg_ref, o_ref, h_scr, a_scr, *, tm, final):
    i = pl.program_id(1)
    last = pl.num_programs(1) - 1
    g = g_ref[...]
    sc = sc_ref[0, 0]
    sh = sh_ref[0, 0]
    x = x_ref[0]
    hp = jnp.where(i > 0, _norm_mod(xp_ref[0], g, sc, sh), 0.0)
    hn = jnp.where(i < last, _norm_mod(xn_ref[0], g, sc, sh), 0.0)
    h_scr[0:HALO] = hp.astype(BF16)
    h_scr[HALO:HALO + tm] = _norm_mod(x, g, sc, sh).astype(BF16)
    h_scr[HALO + tm:] = hn.astype(BF16)
    h = h_scr[...]
    ext = tm + 2 * HALO

    def conv(u, off):
        w0 = cw_ref[0:1, off:off + FFN_CHUNK]
        w1 = cw_ref[1:2, off:off + FFN_CHUNK]
        w2 = cw_ref[2:3, off:off + FFN_CHUNK]
        prev = pltpu.roll(u, 1, axis=0)
        nxt = pltpu.roll(u, ext - 1, axis=0)
        out = w0 * prev + w1 * u + w2 * nxt + cb_ref[:, off:off + FFN_CHUNK]
        return out[HALO:HALO + tm]

    for f in range(D_FF // FFN_CHUNK):
        og = f * FFN_CHUNK
        ov = D_FF + f * FFN_CHUNK
        ug = conv(jnp.dot(h, wup_ref[:, og:og + FFN_CHUNK], preferred_element_type=F32), og)
        uv = conv(jnp.dot(h, wup_ref[:, ov:ov + FFN_CHUNK], preferred_element_type=F32), ov)
        a_scr[:, og:og + FFN_CHUNK] = (ug * jax.nn.sigmoid(ug) * uv).astype(BF16)

    y = jnp.dot(a_scr[...], wdn_ref[...], preferred_element_type=F32)
    out = x + gate_ref[0, 0] * y
    if final:
        ms = jnp.mean(out * out, axis=-1, keepdims=True)
        out = out * lax.rsqrt(ms + EPS) * fg_ref[...]
    o_ref[0] = out


def _ffn(x, g, mod4, w_up, conv_w, conv_b, w_down, final_g, *, tm, final):
    bsz, s, d = x.shape
    hb = tm // HALO
    nh = s // HALO
    return pl.pallas_call(
        functools.partial(_ffn_kernel, tm=tm, final=final),
        grid=(bsz, s // tm),
        in_specs=[pl.BlockSpec((1, tm, d), lambda b, i: (b, i, 0)),
                  pl.BlockSpec((1, HALO, d), lambda b, i: (b, jnp.maximum(i * hb - 1, 0), 0)),
                  pl.BlockSpec((1, HALO, d), lambda b, i: (b, jnp.minimum((i + 1) * hb, nh - 1), 0)),
                  _resident((1, d)),
                  pl.BlockSpec((1, 1, 1, d), lambda b, i: (b, 4, 0, 0)),
                  pl.BlockSpec((1, 1, 1, d), lambda b, i: (b, 3, 0, 0)),
                  pl.BlockSpec((1, 1, 1, d), lambda b, i: (b, 5, 0, 0)),
                  _resident((d, 2 * D_FF)),
                  _resident((3, 2 * D_FF)),
                  _resident((1, 2 * D_FF)),
                  _resident((D_FF, d)),
                  _resident((1, d))],
        out_specs=pl.BlockSpec((1, tm, d), lambda b, i: (b, i, 0)),
        out_shape=jax.ShapeDtypeStruct((bsz, s, d), F32),
        scratch_shapes=[pltpu.VMEM((tm + 2 * HALO, d), BF16),
                        pltpu.VMEM((tm, D_FF), BF16)],
        compiler_params=_cparams(2),
        name="ffn_final" if final else "ffn",
    )(x, x, x, g.reshape(1, d), mod4, mod4, mod4, w_up, conv_w, conv_b.reshape(1, -1), w_down,
      final_g.reshape(1, d))


def _t5_bucket(rel):
    nb = NUM_BUCKETS // 2
    max_exact = nb // 2
    ret = jnp.where(rel > 0, nb, 0)
    n = jnp.abs(rel)
    nf = jnp.maximum(n, 1).astype(jnp.float32)
    large = max_exact + (jnp.log(nf / max_exact) / math.log(MAX_DISTANCE / max_exact)
                         * (nb - max_exact)).astype(jnp.int32)
    large = jnp.minimum(large, nb - 1)
    return ret + jnp.where(n < max_exact, n, large)


N_EDGE = 3


def _bias_kernel(bucket_ref, rb_ref, o_ref):
    var = pl.program_id(0)
    hh = pl.program_id(1)
    head = (hh % A_KV) * A_GROUP + hh // A_KV
    bucket = bucket_ref[...]
    c = lax.broadcasted_iota(jnp.int32, bucket.shape, 0)
    q = lax.broadcasted_iota(jnp.int32, bucket.shape, 1)
    ok = jnp.abs(c - WINDOW - q) <= WINDOW
    ok = ok & ((c >= WINDOW) | (var != 0)) & ((c < WINDOW + A_BLOCK) | (var != N_EDGE - 1))
    val = jnp.zeros(bucket.shape, F32)
    for kb in range(NUM_BUCKETS):
        val = jnp.where(bucket == kb, rb_ref[kb, head], val)
    o_ref[0, 0] = jnp.where(ok, val * LOG2E, -jnp.inf)


def _bias_table(rel_bias):
    c_idx = jnp.arange(A_SPAN)[:, None]
    q_idx = jnp.arange(A_BLOCK)[None, :]
    bucket = _t5_bucket(c_idx - WINDOW - q_idx).astype(jnp.int32)
    return pl.pallas_call(
        _bias_kernel,
        grid=(N_EDGE, A_HEADS),
        in_specs=[pl.BlockSpec((A_SPAN, A_BLOCK), lambda v, h: (0, 0)),
                  pl.BlockSpec(memory_space=pltpu.SMEM)],
        out_specs=pl.BlockSpec((1, 1, A_SPAN, A_BLOCK), lambda v, h: (v, h, 0, 0)),
        out_shape=jax.ShapeDtypeStruct((N_EDGE, A_HEADS, A_SPAN, A_BLOCK), F32),
        compiler_params=_cparams(2),
        name="bias_table",
    )(bucket, rel_bias)


def _swa_kernel(qt_ref, kp_ref, kc_ref, kn_ref, vp_ref, vc_ref, vn_ref, bias_ref, sink_ref, o_ref):
    j = pl.program_id(1)
    last = pl.num_programs(1) - 1
    n_sub = 2
    kblk = [kp_ref[0], kc_ref[0, 0:A_BLOCK], kc_ref[0, A_BLOCK:2 * A_BLOCK], kn_ref[0]]
    vblk = [vp_ref[0, 0], vc_ref[0, 0], vc_ref[0, 1], vn_ref[0, 0]]
    var = [jnp.where(j == 0, 0, 1), jnp.where(j == last, N_EDGE - 1, 1)]
    lane_kv = lax.broadcasted_iota(jnp.int32, (A_SPAN, A_KV * A_HD), 1) // A_HD
    row_kv = lax.broadcasted_iota(jnp.int32, (A_KV * A_HD, A_SPAN), 0) // A_HD
    grp = lax.broadcasted_iota(jnp.int32, (1, A_GROUP * A_BLOCK), 1) // A_BLOCK
    groups = range(A_GROUP)
    kcat = []
    vstack = []
    qp = []
    for u in range(n_sub):
        kcat.append(jnp.concatenate(kblk[u:u + 3], axis=0))
        vcat = jnp.concatenate(vblk[u:u + 3], axis=1)
        vstack.append(jnp.concatenate([jnp.where(row_kv == kv, vcat, jnp.zeros_like(vcat))
                                       for kv in range(A_KV)], axis=1))
        qp.append(jnp.concatenate([qt_ref[0, u, g * MXU_DIM:(g + 1) * MXU_DIM, :] for g in groups],
                                  axis=1))
    ps = [[] for _ in range(n_sub)]
    rden = [[] for _ in range(n_sub)]
    for kv in range(A_KV):
        sink = jnp.zeros((1, A_GROUP * A_BLOCK), F32)
        for g in groups:
            sink = jnp.where(grp == g, sink_ref[kv * A_GROUP + g], sink)
        sink = sink * LOG2E
        for u in range(n_sub):
            bias = jnp.concatenate([bias_ref[var[u], g * A_KV + kv] for g in groups], axis=1)
            k_kv = jnp.where(lane_kv == kv, kcat[u], jnp.zeros_like(kcat[u]))
            s = jnp.dot(k_kv, qp[u], preferred_element_type=F32) + bias
            m = jnp.maximum(jnp.max(s, axis=0, keepdims=True), sink)
            p = jnp.exp2(s - m)
            rden[u].append(1.0 / (jnp.sum(p, axis=0, keepdims=True) + jnp.exp2(sink - m)))
            ps[u].append(p.astype(BF16))
    for u in range(n_sub):
        out = jnp.dot(vstack[u], jnp.concatenate(ps[u], axis=0), preferred_element_type=F32)
        out = jnp.concatenate([out[kv * A_HD:(kv + 1) * A_HD] * rden[u][kv] for kv in range(A_KV)], axis=0)
        for g in groups:
            o_ref[0, u, g * MXU_DIM:(g + 1) * MXU_DIM, :] = (
                out[:, g * A_BLOCK:(g + 1) * A_BLOCK].astype(BF16))


def _swa(k_tok, qv_t, bias_tbl, sink):
    bsz, s, _ = k_tok.shape
    nb = s // A_BLOCK
    assert nb % 2 == 0 and nb >= 4
    qw = A_HEADS * A_HD
    vrow = qw // MXU_DIM
    prev = lambda j: jnp.maximum(2 * j - 1, 0)
    nxt = lambda j: jnp.minimum(2 * j + 2, nb - 1)
    k_edge = lambda row: pl.BlockSpec((1, A_BLOCK, MXU_DIM), lambda b, j: (b, row(j), 0))
    v_edge = lambda row: pl.BlockSpec((1, 1, MXU_DIM, A_BLOCK), lambda b, j: (b, row(j), vrow, 0))
    return pl.pallas_call(
        _swa_kernel,
        grid=(bsz, nb // 2),
        in_specs=[pl.BlockSpec((1, 2, qw, A_BLOCK), lambda b, j: (b, j, 0, 0)),
                  k_edge(prev), pl.BlockSpec((1, 2 * A_BLOCK, MXU_DIM), lambda b, j: (b, j, 0)), k_edge(nxt),
                  v_edge(prev), pl.BlockSpec((1, 2, MXU_DIM, A_BLOCK), lambda b, j: (b, j, vrow, 0)),
                  v_edge(nxt),
                  _resident((N_EDGE, A_HEADS, A_SPAN, A_BLOCK)),
                  pl.BlockSpec(memory_space=pltpu.SMEM)],
        out_specs=pl.BlockSpec((1, 2, qw, A_BLOCK), lambda b, j: (b, j, 0, 0)),
        out_shape=jax.ShapeDtypeStruct((bsz, nb, qw, A_BLOCK), BF16),
        compiler_params=_cparams(2),
        name="swa",
    )(qv_t, k_tok, k_tok, k_tok, qv_t, qv_t, qv_t, bias_tbl, sink)


def _prep_weights(mlstm_w_in, mlstm_b_gate, mlstm_w_out, attn_w_in, attn_w_out, ffn_w_up, ffn_w_down):
    qk = M_HEADS * M_DK
    w_m = mlstm_w_in[0]
    w_q = w_m[:, :qk] * (M_DK ** -0.5)
    w_t = jnp.concatenate([w_q, w_m[:, 2 * qk:4 * qk]], axis=1).T.astype(BF16)
    w_k = w_m[:, qk:2 * qk].astype(BF16)
    perm = lambda g: g.reshape(-1, 2, 2, M_HEADS).swapaxes(1, 2).reshape(-1, 4 * M_HEADS)
    w_gate = jnp.zeros((D_MODEL, GATE_PAD), F32).at[:, :4 * M_HEADS].set(perm(w_m[:, 4 * qk:]))
    b_gate = jnp.zeros((1, GATE_PAD), F32).at[:, :4 * M_HEADS].set(perm(mlstm_b_gate[0][None]))
    qw = A_HEADS * A_HD
    kw = A_KV * A_HD
    w_a = attn_w_in[0]
    wq = w_a[:, :qw].reshape(D_MODEL, A_KV, A_GROUP, A_HD).transpose(0, 2, 1, 3).reshape(D_MODEL, qw)
    w_attn_k = w_a[:, qw:qw + kw].astype(BF16)
    w_attn_t = jnp.concatenate([wq * (A_HD ** -0.5 * LOG2E), w_a[:, qw + kw:]], axis=1).T.astype(BF16)
    wo_attn = attn_w_out[0].reshape(A_KV, A_GROUP, A_HD, D_MODEL).transpose(1, 0, 2, 3)
    wo_attn = wo_attn.reshape(qw, D_MODEL).astype(BF16)
    return dict(w_k=w_k, w_t=w_t, w_gate=w_gate.astype(BF16), b_gate=b_gate,
                wo_m=mlstm_w_out[0].astype(BF16), w_attn_k=w_attn_k, w_attn_t=w_attn_t, wo_attn=wo_attn,
                w_up=ffn_w_up.astype(BF16), w_down=ffn_w_down.astype(BF16))


def _trunk(x, mod, wts, bias_tbl, norm_g, mlstm_head_g, attn_sink, ffn_conv_w, ffn_conv_b, final_g,
           *, tm, tm_ffn):
    bsz, s, d = x.shape
    mod4 = mod[0].reshape(bsz, 6, 1, d)
    k_tok, qvo_t, gates = _inproj_split(x, norm_g[0, 0], mod4, wts["w_k"], wts["w_t"], wts["w_gate"],
                                        tm=tm, chunk=SCAN_CHUNK, tn=512)
    prep = _gateprep(gates, wts["b_gate"], SCAN_CHUNK)
    a_t = _mlstm_scan(k_tok, qvo_t, prep, mlstm_head_g[0], SCAN_CHUNK)
    x = _outproj(a_t, wts["wo_m"], x, mod4, 2, tm=tm, chunk=SCAN_CHUNK)
    x = _ffn(x, norm_g[0, 1], mod4, wts["w_up"][0], ffn_conv_w[0], ffn_conv_b[0], wts["w_down"][0],
             final_g, tm=tm_ffn, final=False)
    mod4 = mod[1].reshape(bsz, 6, 1, d)
    k_tok, qv_t = _inproj_split(x, norm_g[1, 0], mod4, wts["w_attn_k"], wts["w_attn_t"],
                                tm=tm, chunk=A_BLOCK, tn=256)
    a_t = _swa(k_tok, qv_t, bias_tbl, attn_sink[0])
    x = _outproj(a_t, wts["wo_attn"], x, mod4, 2, tm=tm, chunk=A_BLOCK)
    return _ffn(x, norm_g[1, 1], mod4, wts["w_up"][1], ffn_conv_w[1], ffn_conv_b[1], wts["w_down"][1],
                final_g, tm=tm_ffn, final=True)


def kernel(x_prompt, x_sample, c_prompt, c_sample, adaln_w, adaln_b, norm_g, mlstm_w_in, mlstm_b_gate, mlstm_head_g, mlstm_w_out, attn_w_in, attn_sink, attn_w_out, rel_bias, ffn_w_up, ffn_conv_w, ffn_conv_b, ffn_w_down, final_g):
    nbp = x_prompt.shape[0]
    mod = _adaln(jnp.concatenate([c_prompt, c_sample], axis=0), adaln_w, adaln_b)
    wts = _prep_weights(mlstm_w_in, mlstm_b_gate, mlstm_w_out, attn_w_in, attn_w_out, ffn_w_up, ffn_w_down)
    bias_tbl = _bias_table(rel_bias)
    run = functools.partial(_trunk, wts=wts, bias_tbl=bias_tbl, norm_g=norm_g,
                            mlstm_head_g=mlstm_head_g, attn_sink=attn_sink, ffn_conv_w=ffn_conv_w,
                            ffn_conv_b=ffn_conv_b, final_g=final_g, tm=512, tm_ffn=1024)
    return run(x_prompt, mod[:, :nbp]), run(x_sample, mod[:, nbp:])
```

```python
import functools
import math

import jax
import jax.numpy as jnp
from jax import lax
from jax.experimental import pallas as pl
from jax.experimental.pallas import tpu as pltpu

F32 = jnp.float32
BF16 = jnp.bfloat16

D_MODEL = 1024
DEPTH = 2
M_HEADS = 4
M_DK = 256
M_DV = 256
A_HEADS = 16
A_KV = 4
A_GROUP = A_HEADS // A_KV
A_HD = 64
WINDOW = 128
A_BLOCK = 128
A_SPAN = A_BLOCK + 2 * WINDOW
NUM_BUCKETS = 32
MAX_DISTANCE = 128
D_FF = 2816
EPS = 1e-6
LOG2E = 1.4426950408889634

LANES = 128
MXU_DIM = 256
VMEM_LIMIT = 56 * 1024 * 1024

SCAN_CHUNK = 256
GATE_PAD = LANES
HALO = 16
FFN_CHUNK = MXU_DIM


def _cparams(n_axes):
    return pltpu.CompilerParams(dimension_semantics=("parallel",) * n_axes,
                                vmem_limit_bytes=VMEM_LIMIT)


def _resident(shape):
    nd = len(shape)
    return pl.BlockSpec(shape, lambda *_: (0,) * nd, pipeline_mode=pl.Buffered(1))


def _norm_mod(x, g, sc, sh):
    ms = jnp.mean(x * x, axis=-1, keepdims=True)
    y = x * lax.rsqrt(ms + EPS) * g
    return y * (1.0 + sc) + sh


def _adaln_kernel(c_ref, w_ref, b_ref, o_ref):
    c = c_ref[...]
    s = c * jax.nn.sigmoid(c)
    o_ref[0] = jnp.dot(s, w_ref[0], precision=lax.Precision.HIGHEST,
                       preferred_element_type=F32) + b_ref[0]


def _adaln(c_all, adaln_w, adaln_b):
    nb = c_all.shape[0]
    n = adaln_w.shape[-1]
    tn = 1536
    return pl.pallas_call(
        _adaln_kernel,
        grid=(DEPTH, n // tn),
        in_specs=[pl.BlockSpec((nb, D_MODEL), lambda l, j: (0, 0)),
                  pl.BlockSpec((1, D_MODEL, tn), lambda l, j: (l, 0, j)),
                  pl.BlockSpec((1, 1, tn), lambda l, j: (l, 0, j))],
        out_specs=pl.BlockSpec((1, nb, tn), lambda l, j: (l, 0, j)),
        out_shape=jax.ShapeDtypeStruct((DEPTH, nb, n), F32),
        compiler_params=_cparams(2),
        name="adaln",
    )(c_all, adaln_w, adaln_b.reshape(DEPTH, 1, n))


def _inproj_split_kernel(x_ref, g_ref, sc_ref, sh_ref, wk_ref, wt_ref, *rest, tn, chunk, with_gates):
    if with_gates:
        wg_ref, k_ref, t_ref, gate_ref = rest
    else:
        k_ref, t_ref = rest
    h = _norm_mod(x_ref[0], g_ref[...], sc_ref[0, 0], sh_ref[0, 0]).astype(BF16)
    tm = h.shape[0]
    nk = wk_ref.shape[1]
    tk = min(tn, nk)
    for j in range(nk // tk):
        k_ref[0, :, j * tk:(j + 1) * tk] = jnp.dot(
            h, wk_ref[:, j * tk:(j + 1) * tk], preferred_element_type=F32).astype(BF16)
    if with_gates:
        gate_ref[0] = jnp.dot(h, wg_ref[...], preferred_element_type=F32)
    for j in range(wt_ref.shape[0] // tn):
        res = lax.dot_general(wt_ref[j * tn:(j + 1) * tn, :], h, (((1,), (1,)), ((), ())),
                              preferred_element_type=F32).astype(BF16)
        for cc in range(tm // chunk):
            t_ref[0, cc, j * tn:(j + 1) * tn, :] = res[:, cc * chunk:(cc + 1) * chunk]


def _inproj_split(x, g, mod4, w_k, w_t, w_gate=None, *, tm, chunk, tn):
    bsz, s, d = x.shape
    nk = w_k.shape[1]
    nt = w_t.shape[0]
    with_gates = w_gate is not None
    in_specs = [pl.BlockSpec((1, tm, d), lambda b, i: (b, i, 0)),
                _resident((1, d)),
                pl.BlockSpec((1, 1, 1, d), lambda b, i: (b, 1, 0, 0)),
                pl.BlockSpec((1, 1, 1, d), lambda b, i: (b, 0, 0, 0)),
                _resident((d, nk)), _resident((nt, d))]
    args = [x, g.reshape(1, d), mod4, mod4, w_k, w_t]
    out_specs = [pl.BlockSpec((1, tm, nk), lambda b, i: (b, i, 0)),
                 pl.BlockSpec((1, tm // chunk, nt, chunk), lambda b, i: (b, i, 0, 0))]
    out_shape = [jax.ShapeDtypeStruct((bsz, s, nk), BF16),
                 jax.ShapeDtypeStruct((bsz, s // chunk, nt, chunk), BF16)]
    if with_gates:
        in_specs.append(_resident((d, GATE_PAD)))
        args.append(w_gate)
        out_specs.append(pl.BlockSpec((1, tm, GATE_PAD), lambda b, i: (b, i, 0)))
        out_shape.append(jax.ShapeDtypeStruct((bsz, s, GATE_PAD), F32))
    return pl.pallas_call(
        functools.partial(_inproj_split_kernel, tn=tn, chunk=chunk, with_gates=with_gates),
        grid=(bsz, s // tm),
        in_specs=in_specs, out_specs=out_specs, out_shape=out_shape,
        compiler_params=_cparams(2),
        name="inproj_m" if with_gates else "inproj_a",
    )(*args)


N_PREP = 6


def _gateprep_kernel(g_ref, b_ref, p_ref, *, nc, chunk):
    s = nc * chunk
    gt = jnp.concatenate(
        [(g_ref[0, r * LANES:(r + 1) * LANES, :] + b_ref[...]).T[0:16] for r in range(s // LANES)],
        axis=1)
    ig = gt[0:8]
    fg = gt[8:16]
    row = lax.broadcasted_iota(jnp.int32, (8, s), 0)
    pos = lax.broadcasted_iota(jnp.int32, (8, s), 1) & (chunk - 1)
    fwd = row < M_HEADS

    def seg_scan(x, op, ident):
        k = 1
        while k < chunk:
            sh_f = jnp.where(pos >= k, pltpu.roll(x, k, axis=1), ident)
            sh_b = jnp.where(pos < chunk - k, pltpu.roll(x, s - k, axis=1), ident)
            x = op(x, jnp.where(fwd, sh_f, sh_b))
            k *= 2
        return x

    lf = jnp.minimum(fg, 0.0) - jnp.log1p(jnp.exp(-jnp.abs(fg)))
    bsum = seg_scan(lf, jnp.add, 0.0)
    beta = ig - bsum
    cmax = seg_scan(beta, jnp.maximum, -jnp.inf)

    fwd1 = fwd[:, :1]
    g_max = [jnp.max(beta[:, c * chunk:(c + 1) * chunk], axis=1, keepdims=True) for c in range(nc)]
    b_last = [jnp.where(fwd1, bsum[:, (c + 1) * chunk - 1:(c + 1) * chunk], bsum[:, c * chunk:c * chunk + 1])
              for c in range(nc)]
    m_f = [None] * nc
    m_b = [None] * nc
    mf = jnp.zeros((8, 1), F32)
    mb = jnp.zeros((8, 1), F32)
    for i in range(nc):
        j = nc - 1 - i
        m_f[i] = mf
        m_b[j] = mb
        mf = b_last[i] + jnp.maximum(mf, g_max[i])
        mb = b_last[j] + jnp.maximum(mb, g_max[j])
    wide = lambda v: jnp.broadcast_to(v, (8, chunk))
    m_in = jnp.concatenate([wide(jnp.where(fwd1, m_f[c], m_b[c])) for c in range(nc)], axis=1)
    g_all = jnp.concatenate([wide(g_max[c]) for c in range(nc)], axis=1)

    mx = jnp.maximum(cmax, m_in)
    m_c = jnp.maximum(m_in, g_all)
    quantities = (beta * LOG2E, mx * LOG2E, jnp.exp(m_in - mx), jnp.exp(-(bsum + mx)), jnp.exp(beta - m_c),
                  jnp.exp(m_in - m_c))
    row16 = lax.broadcasted_iota(jnp.int32, (16, s), 0)
    for h in range(M_HEADS):
        tile = jnp.zeros((16, s), F32)
        for d in range(2):
            for qi, val in enumerate(quantities):
                src = val[d * M_HEADS + h:d * M_HEADS + h + 1]
                tile = jnp.where(row16 == d * 8 + qi, jnp.broadcast_to(src, (16, s)), tile)
        for c in range(nc):
            p_ref[0, c, h] = tile[:, c * chunk:(c + 1) * chunk]


def _gateprep(gates, bias, chunk):
    bsz, s, _ = gates.shape
    nc = s // chunk
    return pl.pallas_call(
        functools.partial(_gateprep_kernel, nc=nc, chunk=chunk),
        grid=(bsz,),
        in_specs=[pl.BlockSpec((1, s, GATE_PAD), lambda b: (b, 0, 0)),
                  _resident((1, GATE_PAD))],
        out_specs=pl.BlockSpec((1, nc, M_HEADS, 16, chunk), lambda b: (b, 0, 0, 0, 0)),
        out_shape=jax.ShapeDtypeStruct((bsz, nc, M_HEADS, 16, chunk), F32),
        compiler_params=_cparams(1),
        name="gateprep",
    )(gates, bias)


def _mlstm_kernel(qt_ref, vt_ref, ot_ref, k_ref, p_ref, hg_ref, at_ref, acc, ct_scr, n_scr,
                  *, nc, chunk):
    L = chunk
    ct_scr[...] = jnp.zeros_like(ct_scr)
    n_scr[...] = jnp.zeros_like(n_scr)
    s_idx = lax.broadcasted_iota(jnp.int32, (L, L), 0)
    t_idx = lax.broadcasted_iota(jnp.int32, (L, L), 1)
    tri = (s_idx <= t_idx, s_idx >= t_idx)
    n_pad = 16

    def body(i, carry, *, finish):
        cs = (i, nc - 1 - i)
        qt = [qt_ref[0, c] for c in cs]
        vt = [vt_ref[0, c] for c in cs]
        k = [k_ref[0, pl.ds(pl.multiple_of(c * L, L), L), :] for c in cs]
        prm = [p_ref[0, cs[d], 0, d * 8:(d + 1) * 8, :] for d in range(2)]
        ct_old = [ct_scr[d] for d in range(2)]
        n_old = [n_scr[d] for d in range(2)]
        r1 = []
        for d in range(2):
            lhs = jnp.concatenate([k[d], ct_old[d].astype(BF16),
                                   jnp.concatenate([n_old[d], n_old[d]], axis=0).astype(BF16)], axis=0)
            r1.append(jnp.dot(lhs, qt[d], preferred_element_type=F32))
        r2 = []
        for d in range(2):
            w = prm[d][4:5]
            vts = (vt[d].astype(F32) * w).astype(BF16)
            lhs2 = jnp.concatenate([vts, jnp.broadcast_to(w, (n_pad, L)).astype(BF16)], axis=0)
            r2.append(jnp.dot(lhs2, k[d], preferred_element_type=F32))
        for d in range(2):
            beta, mx, w_inter, neg_mt, _, decay = (prm[d][j:j + 1] for j in range(N_PREP))
            d_t = jnp.where(tri[d], jnp.exp2(jnp.broadcast_to(beta, (L, L)).T - mx), 0.0)
            p_t = r1[d][0:L] * d_t
            inter = r1[d][L:L + M_DV]
            qn = r1[d][L + M_DV:L + M_DV + 1]
            num = jnp.dot(vt[d], p_t.astype(BF16), preferred_element_type=F32) + w_inter * inter
            den = jnp.sum(p_t, axis=0, keepdims=True) + w_inter * qn
            hout = num * (1.0 / jnp.maximum(jnp.abs(den), neg_mt))
            if finish:
                hs = acc[cs[d]] + hout
                ms = jnp.mean(hs * hs, axis=0, keepdims=True)
                y = hs * lax.rsqrt(ms + EPS) * hg_ref[0]
                gate = jax.nn.sigmoid(ot_ref[0, cs[d]].astype(F32))
                at_ref[0, cs[d]] = (y * gate).astype(BF16)
            else:
                acc[cs[d]] = hout
            ct_scr[d] = decay * ct_old[d] + r2[d][0:M_DV]
            n_scr[d] = decay * n_old[d] + r2[d][M_DV:M_DV + 8]
        return carry

    half = nc // 2
    lax.fori_loop(0, half, functools.partial(body, finish=False), 0, unroll=2)
    lax.fori_loop(half, nc, functools.partial(body, finish=True), 0, unroll=2)


def _mlstm_scan(k_tok, qvo_t, prep, head_g, chunk):
    bsz, s, _ = k_tok.shape
    nc = s // chunk
    assert nc % 4 == 0
    blk_t = lambda off: pl.BlockSpec((1, nc, M_DK, chunk), lambda b, h: (b, 0, off + h, 0))
    return pl.pallas_call(
        functools.partial(_mlstm_kernel, nc=nc, chunk=chunk),
        grid=(bsz, M_HEADS),
        in_specs=[blk_t(0), blk_t(M_HEADS), blk_t(2 * M_HEADS),
                  pl.BlockSpec((1, s, M_DK), lambda b, h: (b, 0, h)),
                  pl.BlockSpec((1, nc, 1, 16, chunk), lambda b, h: (b, 0, h, 0, 0)),
                  pl.BlockSpec((1, M_DV, 1), lambda b, h: (h, 0, 0))],
        out_specs=pl.BlockSpec((1, nc, M_DV, chunk), lambda b, h: (b, 0, h, 0)),
        out_shape=jax.ShapeDtypeStruct((bsz, nc, M_HEADS * M_DV, chunk), BF16),
        scratch_shapes=[pltpu.VMEM((nc, M_DV, chunk), F32),
                        pltpu.VMEM((2, M_DV, M_DK), F32),
                        pltpu.VMEM((2, 8, M_DK), F32)],
        compiler_params=_cparams(2),
        name="mlstm_scan",
    )(qvo_t, qvo_t, qvo_t, k_tok, prep, head_g.reshape(M_HEADS, M_DV, 1))


def _mix_ffn_kernel(a_ref, ap_ref, an_ref, x_ref, xp_ref, xn_ref, wo_ref, g1_ref, g_ref, sc_ref, sh_ref,
                    gate_ref, wup_ref, cw_ref, cb_ref, wdn_ref, fg_ref, o_ref, h_scr, a_scr,
                    *, tm, final):
    i = pl.program_id(1)
    last = pl.num_programs(1) - 1
    wo = wo_ref[...]
    g1 = g1_ref[0, 0]

    def mix(a_t):
        return lax.dot_general(a_t, wo, (((0,), (0,)), ((), ())), preferred_element_type=F32)

    a_main = jnp.concatenate([a_ref[0, cc] for cc in range(a_ref.shape[1])], axis=1)
    x1 = x_ref[0] + g1 * mix(a_main)
    x1p = xp_ref[0] + g1 * mix(ap_ref[0, 0][:, LANES - HALO:])
    x1n = xn_ref[0] + g1 * mix(an_ref[0, 0][:, :HALO])

    g = g_ref[...]
    sc = sc_ref[0, 0]
    sh = sh_ref[0, 0]
    hp = jnp.where(i > 0, _norm_mod(x1p, g, sc, sh), 0.0)
    hn = jnp.where(i < last, _norm_mod(x1n, g, sc, sh), 0.0)
    h_scr[0:HALO] = hp.astype(BF16)
    h_scr[HALO:HALO + tm] = _norm_mod(x1, g, sc, sh).astype(BF16)
    h_scr[HALO + tm:] = hn.astype(BF16)
    h = h_scr[...]
    ext = tm + 2 * HALO

    def conv(u, off):
        w0 = cw_ref[0:1, off:off + FFN_CHUNK]
        w1 = cw_ref[1:2, off:off + FFN_CHUNK]
        w2 = cw_ref[2:3, off:off + FFN_CHUNK]
        prev = pltpu.roll(u, 1, axis=0)
        nxt = pltpu.roll(u, ext - 1, axis=0)
        out = w0 * prev + w1 * u + w2 * nxt + cb_ref[:, off:off + FFN_CHUNK]
        return out[HALO:HALO + tm]

    for f in range(D_FF // FFN_CHUNK):
        og = f * FFN_CHUNK
        ov = D_FF + f * FFN_CHUNK
        ug = conv(jnp.dot(h, wup_ref[:, og:og + FFN_CHUNK], preferred_element_type=F32), og)
        uv = conv(jnp.dot(h, wup_ref[:, ov:ov + FFN_CHUNK], preferred_element_type=F32), ov)
        a_scr[:, og:og + FFN_CHUNK] = (ug * jax.nn.sigmoid(ug) * uv).astype(BF16)

    y = jnp.dot(a_scr[...], wdn_ref[...], preferred_element_type=F32)
    out = x1 + gate_ref[0, 0] * y
    if final:
        ms = jnp.mean(out * out, axis=-1, keepdims=True)
        out = out * lax.rsqrt(ms + EPS) * fg_ref[...]
    o_ref[0] = out


def _mix_ffn(a_t, w_o, x, g, mod4, w_up, conv_w, conv_b, w_down, final_g, *, tm, chunk, final):
    bsz, s, d = x.shape
    kdim = w_o.shape[0]
    nc = s // chunk
    cpt = tm // chunk
    hb = tm // HALO
    nh = s // HALO
    return pl.pallas_call(
        functools.partial(_mix_ffn_kernel, tm=tm, final=final),
        grid=(bsz, s // tm),
        in_specs=[pl.BlockSpec((1, cpt, kdim, chunk), lambda b, i: (b, i, 0, 0)),
                  pl.BlockSpec((1, 1, kdim, LANES),
                               lambda b, i: (b, jnp.maximum(i * cpt - 1, 0), 0, chunk // LANES - 1)),
                  pl.BlockSpec((1, 1, kdim, LANES),
                               lambda b, i: (b, jnp.minimum((i + 1) * cpt, nc - 1), 0, 0)),
                  pl.BlockSpec((1, tm, d), lambda b, i: (b, i, 0)),
                  pl.BlockSpec((1, HALO, d), lambda b, i: (b, jnp.maximum(i * hb - 1, 0), 0)),
                  pl.BlockSpec((1, HALO, d), lambda b, i: (b, jnp.minimum((i + 1) * hb, nh - 1), 0)),
                  _resident((kdim, d)),
                  pl.BlockSpec((1, 1, 1, d), lambda b, i: (b, 2, 0, 0)),
                  _resident((1, d)),
                  pl.BlockSpec((1, 1, 1, d), lambda b, i: (b, 4, 0, 0)),
                  pl.BlockSpec((1, 1, 1, d), lambda b, i: (b, 3, 0, 0)),
                  pl.BlockSpec((1, 1, 1, d), lambda b, i: (b, 5, 0, 0)),
                  _resident((d, 2 * D_FF)),
                  _resident((3, 2 * D_FF)),
                  _resident((1, 2 * D_FF)),
                  _resident((D_FF, d)),
                  _resident((1, d))],
        out_specs=pl.BlockSpec((1, tm, d), lambda b, i: (b, i, 0)),
        out_shape=jax.ShapeDtypeStruct((bsz, s, d), F32),
        scratch_shapes=[pltpu.VMEM((tm + 2 * HALO, d), BF16),
                        pltpu.VMEM((tm, D_FF), BF16)],
        compiler_params=_cparams(2),
        name="mix_ffn_final" if final else "mix_ffn",
    )(a_t, a_t, a_t, x, x, x, w_o, mod4, g.reshape(1, d), mod4, mod4, mod4, w_up, conv_w,
      conv_b.reshape(1, -1), w_down, final_g.reshape(1, d))


def _t5_bucket(rel):
    nb = NUM_BUCKETS // 2
    max_exact = nb // 2
    ret = jnp.where(rel > 0, nb, 0)
    n = jnp.abs(rel)
    nf = jnp.maximum(n, 1).astype(jnp.float32)
    large = max_exact + (jnp.log(nf / max_exact) / math.log(MAX_DISTANCE / max_exact)
                         * (nb - max_exact)).astype(jnp.int32)
    large = jnp.minimum(large, nb - 1)
    return ret + jnp.where(n < max_exact, n, large)


N_EDGE = 3


def _bias_kernel(bucket_ref, rb_ref, o_ref):
    var = pl.program_id(0)
    bucket = bucket_ref[...]
    c = lax.broadcasted_iota(jnp.int32, bucket.shape, 0)
    q = lax.broadcasted_iota(jnp.int32, bucket.shape, 1)
    ok = jnp.abs(c - WINDOW - q) <= WINDOW
    ok = ok & ((c >= WINDOW) | (var != 0)) & ((c < WINDOW + A_BLOCK) | (var != N_EDGE - 1))
    hit = [bucket == kb for kb in range(NUM_BUCKETS)]
    for hh in range(A_HEADS):
        head = (hh % A_KV) * A_GROUP + hh // A_KV
        val = jnp.zeros(bucket.shape, F32)
        for kb in range(NUM_BUCKETS):
            val = jnp.where(hit[kb], rb_ref[kb, head], val)
        o_ref[0, hh] = jnp.where(ok, val * LOG2E, -jnp.inf)


def _bias_table(rel_bias):
    c_idx = jnp.arange(A_SPAN)[:, None]
    q_idx = jnp.arange(A_BLOCK)[None, :]
    bucket = _t5_bucket(c_idx - WINDOW - q_idx).astype(jnp.int32)
    return pl.pallas_call(
        _bias_kernel,
        grid=(N_EDGE,),
        in_specs=[pl.BlockSpec((A_SPAN, A_BLOCK), lambda v: (0, 0)),
                  pl.BlockSpec(memory_space=pltpu.SMEM)],
        out_specs=pl.BlockSpec((1, A_HEADS, A_SPAN, A_BLOCK), lambda v: (v, 0, 0, 0)),
        out_shape=jax.ShapeDtypeStruct((N_EDGE, A_HEADS, A_SPAN, A_BLOCK), F32),
        compiler_params=_cparams(1),
        name="bias_table",
    )(bucket, rel_bias)


def _swa_kernel(qt_ref, kp_ref, kc_ref, kn_ref, vp_ref, vc_ref, vn_ref, bias_ref, sink_ref, o_ref):
    j = pl.program_id(1)
    last = pl.num_programs(1) - 1
    n_sub = 2
    kblk = [kp_ref[0], kc_ref[0, 0:A_BLOCK], kc_ref[0, A_BLOCK:2 * A_BLOCK], kn_ref[0]]
    vblk = [vp_ref[0, 0], vc_ref[0, 0], vc_ref[0, 1], vn_ref[0, 0]]
    var = [jnp.where(j == 0, 0, 1), jnp.where(j == last, N_EDGE - 1, 1)]
    lane_kv = lax.broadcasted_iota(jnp.int32, (A_SPAN, A_KV * A_HD), 1) // A_HD
    row_kv = lax.broadcasted_iota(jnp.int32, (A_KV * A_HD, A_SPAN), 0) // A_HD
    grp = lax.broadcasted_iota(jnp.int32, (1, A_GROUP * A_BLOCK), 1) // A_BLOCK
    groups = range(A_GROUP)
    kcat = []
    vstack = []
    qp = []
    for u in range(n_sub):
        kcat.append(jnp.concatenate(kblk[u:u + 3], axis=0))
        vcat = jnp.concatenate(vblk[u:u + 3], axis=1)
        vstack.append(jnp.concatenate([jnp.where(row_kv == kv, vcat, jnp.zeros_like(vcat))
                                       for kv in range(A_KV)], axis=1))
        qp.append(jnp.concatenate([qt_ref[0, u, g * MXU_DIM:(g + 1) * MXU_DIM, :] for g in groups],
                                  axis=1))
    ps = [[] for _ in range(n_sub)]
    rden = [[] for _ in range(n_sub)]
    for kv in range(A_KV):
        sink = jnp.zeros((1, A_GROUP * A_BLOCK), F32)
        for g in groups:
            sink = jnp.where(grp == g, sink_ref[kv * A_GROUP + g], sink)
        sink = sink * LOG2E
        for u in range(n_sub):
            bias = jnp.concatenate([bias_ref[var[u], g * A_KV + kv] for g in groups], axis=1)
            k_kv = jnp.where(lane_kv == kv, kcat[u], jnp.zeros_like(kcat[u]))
            s = jnp.dot(k_kv, qp[u], preferred_element_type=F32) + bias
            m = jnp.maximum(jnp.max(s, axis=0, keepdims=True), sink)
            p = jnp.exp2(s - m)
            rden[u].append(1.0 / (jnp.sum(p, axis=0, keepdims=True) + jnp.exp2(sink - m)))
            ps[u].append(p.astype(BF16))
    for u in range(n_sub):
        out = jnp.dot(vstack[u], jnp.concatenate(ps[u], axis=0), preferred_element_type=F32)
        out = jnp.concatenate([out[kv * A_HD:(kv + 1) * A_HD] * rden[u][kv] for kv in range(A_KV)], axis=0)
        for g in groups:
            o_ref[0, u, g * MXU_DIM:(g + 1) * MXU_DIM, :] = (
                out[:, g * A_BLOCK:(g + 1) * A_BLOCK].astype(BF16))


def _swa(k_tok, qv_t, bias_tbl, sink):
    bsz, s, _ = k_tok.shape
    nb = s // A_BLOCK
    assert nb % 2 == 0 and nb >= 4
    qw = A_HEADS * A_HD
    vrow = qw // MXU_DIM
    prev = lambda j: jnp.maximum(2 * j - 1, 0)
    nxt = lambda j: jnp.minimum(2 * j + 2, nb - 1)
    k_edge = lambda row: pl.BlockSpec((1, A_BLOCK, MXU_DIM), lambda b, j: (b, row(j), 0))
    v_edge = lambda row: pl.BlockSpec((1, 1, MXU_DIM, A_BLOCK), lambda b, j: (b, row(j), vrow, 0))
    return pl.pallas_call(
        _swa_kernel,
        grid=(bsz, nb // 2),
        in_specs=[pl.BlockSpec((1, 2, qw, A_BLOCK), lambda b, j: (b, j, 0, 0)),
                  k_edge(prev), pl.BlockSpec((1, 2 * A_BLOCK, MXU_DIM), lambda b, j: (b, j, 0)), k_edge(nxt),
                  v_edge(prev), pl.BlockSpec((1, 2, MXU_DIM, A_BLOCK), lambda b, j: (b, j, vrow, 0)),
                  v_edge(nxt),
                  _resident((N_EDGE, A_HEADS, A_SPAN, A_BLOCK)),
                  pl.BlockSpec(memory_space=pltpu.SMEM)],
        out_specs=pl.BlockSpec((1, 2, qw, A_BLOCK), lambda b, j: (b, j, 0, 0)),
        out_shape=jax.ShapeDtypeStruct((bsz, nb, qw, A_BLOCK), BF16),
        compiler_params=_cparams(2),
        name="swa",
    )(qv_t, k_tok, k_tok, k_tok, qv_t, qv_t, qv_t, bias_tbl, sink)


def _prep_weights(mlstm_w_in, mlstm_b_gate, mlstm_w_out, attn_w_in, attn_w_out, ffn_w_up, ffn_w_down):
    qk = M_HEADS * M_DK
    w_m = mlstm_w_in[0]
    w_q = w_m[:, :qk] * (M_DK ** -0.5)
    w_t = jnp.concatenate([w_q, w_m[:, 2 * qk:4 * qk]], axis=1).T.astype(BF16)
    w_k = w_m[:, qk:2 * qk].astype(BF16)
    perm = lambda g: g.reshape(-1, 2, 2, M_HEADS).swapaxes(1, 2).reshape(-1, 4 * M_HEADS)
    w_gate = jnp.zeros((D_MODEL, GATE_PAD), F32).at[:, :4 * M_HEADS].set(perm(w_m[:, 4 * qk:]))
    b_gate = jnp.zeros((1, GATE_PAD), F32).at[:, :4 * M_HEADS].set(perm(mlstm_b_gate[0][None]))
    qw = A_HEADS * A_HD
    kw = A_KV * A_HD
    w_a = attn_w_in[0]
    wq = w_a[:, :qw].reshape(D_MODEL, A_KV, A_GROUP, A_HD).transpose(0, 2, 1, 3).reshape(D_MODEL, qw)
    w_attn_k = w_a[:, qw:qw + kw].astype(BF16)
    w_attn_t = jnp.concatenate([wq * (A_HD ** -0.5 * LOG2E), w_a[:, qw + kw:]], axis=1).T.astype(BF16)
    wo_attn = attn_w_out[0].reshape(A_KV, A_GROUP, A_HD, D_MODEL).transpose(1, 0, 2, 3)
    wo_attn = wo_attn.reshape(qw, D_MODEL).astype(BF16)
    return dict(w_k=w_k, w_t=w_t, w_gate=w_gate.astype(BF16), b_gate=b_gate,
                wo_m=mlstm_w_out[0].astype(BF16), w_attn_k=w_attn_k, w_attn_t=w_attn_t, wo_attn=wo_attn,
                w_up=ffn_w_up.astype(BF16), w_down=ffn_w_down.astype(BF16))


def _trunk(x, mod, wts, bias_tbl, norm_g, mlstm_head_g, attn_sink, ffn_conv_w, ffn_conv_b, final_g,
           *, tm, tm_ffn):
    bsz, s, d = x.shape
    mod4 = mod[0].reshape(bsz, 6, 1, d)
    k_tok, qvo_t, gates = _inproj_split(x, norm_g[0, 0], mod4, wts["w_k"], wts["w_t"], wts["w_gate"],
                                        tm=tm, chunk=SCAN_CHUNK, tn=512)
    prep = _gateprep(gates, wts["b_gate"], SCAN_CHUNK)
    a_t = _mlstm_scan(k_tok, qvo_t, prep, mlstm_head_g[0], SCAN_CHUNK)
    x = _mix_ffn(a_t, wts["wo_m"], x, norm_g[0, 1], mod4, wts["w_up"][0], ffn_conv_w[0], ffn_conv_b[0],
                 wts["w_down"][0], final_g, tm=tm_ffn, chunk=SCAN_CHUNK, final=False)
    mod4 = mod[1].reshape(bsz, 6, 1, d)
    k_tok, qv_t = _inproj_split(x, norm_g[1, 0], mod4, wts["w_attn_k"], wts["w_attn_t"],
                                tm=tm, chunk=A_BLOCK, tn=256)
    a_t = _swa(k_tok, qv_t, bias_tbl, attn_sink[0])
    return _mix_ffn(a_t, wts["wo_attn"], x, norm_g[1, 1], mod4, wts["w_up"][1], ffn_conv_w[1],
                    ffn_conv_b[1], wts["w_down"][1], final_g, tm=tm_ffn, chunk=A_BLOCK, final=True)


def kernel(x_prompt, x_sample, c_prompt, c_sample, adaln_w, adaln_b, norm_g, mlstm_w_in, mlstm_b_gate, mlstm_head_g, mlstm_w_out, attn_w_in, attn_sink, attn_w_out, rel_bias, ffn_w_up, ffn_conv_w, ffn_conv_b, ffn_w_down, final_g):
    nbp = x_prompt.shape[0]
    mod = _adaln(jnp.concatenate([c_prompt, c_sample], axis=0), adaln_w, adaln_b)
    wts = _prep_weights(mlstm_w_in, mlstm_b_gate, mlstm_w_out, attn_w_in, attn_w_out, ffn_w_up, ffn_w_down)
    bias_tbl = _bias_table(rel_bias)
    run = functools.partial(_trunk, wts=wts, bias_tbl=bias_tbl, norm_g=norm_g,
                            mlstm_head_g=mlstm_head_g, attn_sink=attn_sink, ffn_conv_w=ffn_conv_w,
                            ffn_conv_b=ffn_conv_b, final_g=final_g, tm=512, tm_ffn=1024)
    return run(x_prompt, mod[:, :nbp]), run(x_sample, mod[:, nbp:])
```

```python
import functools
import math

import jax
import jax.numpy as jnp
from jax import lax
from jax.experimental import pallas as pl
from jax.experimental.pallas import tpu as pltpu

F32 = jnp.float32
BF16 = jnp.bfloat16

D_MODEL = 1024
DEPTH = 2
M_HEADS = 4
M_DK = 256
M_DV = 256
A_HEADS = 16
A_KV = 4
A_GROUP = A_HEADS // A_KV
A_HD = 64
WINDOW = 128
A_BLOCK = 128
A_SPAN = A_BLOCK + 2 * WINDOW
NUM_BUCKETS = 32
MAX_DISTANCE = 128
D_FF = 2816
EPS = 1e-6
LOG2E = 1.4426950408889634

LANES = 128
MXU_DIM = 256
VMEM_LIMIT = 56 * 1024 * 1024

SCAN_CHUNK = 256
N_GATES = 4 * M_HEADS
HALO = 16
FFN_CHUNK = MXU_DIM
SWA_SUB = 4


def _cparams(n_axes):
    return pltpu.CompilerParams(dimension_semantics=("parallel",) * n_axes,
                                vmem_limit_bytes=VMEM_LIMIT)


def _resident(shape, layer=None):
    nd = len(shape)
    if layer is None:
        return pl.BlockSpec(shape, lambda *_: (0,) * nd, pipeline_mode=pl.Buffered(1))
    return pl.BlockSpec((1,) + tuple(shape), lambda *_: (layer,) + (0,) * nd,
                        pipeline_mode=pl.Buffered(1))


def _norm_mod(x, g, sc, sh):
    ms = jnp.mean(x * x, axis=-1, keepdims=True)
    return x * lax.rsqrt(ms + EPS) * (g * (1.0 + sc)) + sh


def _adaln_kernel(c_ref, w_ref, b_ref, o_ref):
    c = c_ref[...]
    s = c * jax.nn.sigmoid(c)
    o_ref[0] = jnp.dot(s, w_ref[0], precision=lax.Precision.HIGHEST,
                       preferred_element_type=F32) + b_ref[0]


def _adaln(c_all, adaln_w, adaln_b):
    nb = c_all.shape[0]
    n = adaln_w.shape[-1]
    tn = 1536
    return pl.pallas_call(
        _adaln_kernel,
        grid=(DEPTH, n // tn),
        in_specs=[pl.BlockSpec((nb, D_MODEL), lambda l, j: (0, 0)),
                  pl.BlockSpec((1, D_MODEL, tn), lambda l, j: (l, 0, j)),
                  pl.BlockSpec((1, 1, tn), lambda l, j: (l, 0, j))],
        out_specs=pl.BlockSpec((1, nb, tn), lambda l, j: (l, 0, j)),
        out_shape=jax.ShapeDtypeStruct((DEPTH, nb, n), F32),
        compiler_params=_cparams(2),
        name="adaln",
    )(c_all, adaln_w, adaln_b.reshape(DEPTH, 1, n))


def _inproj_split_kernel(x_ref, g_ref, sc_ref, sh_ref, wk_ref, wt_ref, k_ref, t_ref, *rest,
                         tn, chunk, n_extra):
    h = _norm_mod(x_ref[0], g_ref[...], sc_ref[0, 0], sh_ref[0, 0]).astype(BF16)
    tm = h.shape[0]
    nk = wk_ref.shape[1]
    tk = min(tn, nk)
    for j in range(nk // tk):
        k_ref[0, :, j * tk:(j + 1) * tk] = jnp.dot(
            h, wk_ref[:, j * tk:(j + 1) * tk], preferred_element_type=F32).astype(BF16)
    nt = wt_ref.shape[0] - n_extra
    for j in range(nt // tn):
        hi = (j + 1) * tn + (n_extra if j == nt // tn - 1 else 0)
        res = lax.dot_general(wt_ref[j * tn:hi, :], h, (((1,), (1,)), ((), ())),
                              preferred_element_type=F32)
        for cc in range(tm // chunk):
            cols = slice(cc * chunk, (cc + 1) * chunk)
            t_ref[0, cc, j * tn:(j + 1) * tn, :] = res[:tn, cols].astype(BF16)
            if hi > (j + 1) * tn:
                rest[0][0, cc] = res[tn:, cols]


def _inproj_split(x, g, mod4, w_k, w_t, *, tm, chunk, tn, n_extra=0):
    bsz, s, d = x.shape
    nk = w_k.shape[1]
    nt = w_t.shape[0] - n_extra
    out_specs = [pl.BlockSpec((1, tm, nk), lambda b, i: (b, i, 0)),
                 pl.BlockSpec((1, tm // chunk, nt, chunk), lambda b, i: (b, i, 0, 0))]
    out_shape = [jax.ShapeDtypeStruct((bsz, s, nk), BF16),
                 jax.ShapeDtypeStruct((bsz, s // chunk, nt, chunk), BF16)]
    if n_extra:
        out_specs.append(pl.BlockSpec((1, tm // chunk, n_extra, chunk), lambda b, i: (b, i, 0, 0)))
        out_shape.append(jax.ShapeDtypeStruct((bsz, s // chunk, n_extra, chunk), F32))
    return pl.pallas_call(
        functools.partial(_inproj_split_kernel, tn=tn, chunk=chunk, n_extra=n_extra),
        grid=(bsz, s // tm),
        in_specs=[pl.BlockSpec((1, tm, d), lambda b, i: (b, i, 0)),
                  _resident((1, d)),
                  pl.BlockSpec((1, 1, 1, d), lambda b, i: (b, 1, 0, 0)),
                  pl.BlockSpec((1, 1, 1, d), lambda b, i: (b, 0, 0, 0)),
                  _resident((d, nk)), _resident((nt + n_extra, d))],
        out_specs=out_specs, out_shape=out_shape,
        compiler_params=_cparams(2),
        name="inproj_m" if n_extra else "inproj_a",
    )(x, g.reshape(1, d), mod4, mod4, w_k, w_t)


N_PREP = 6


def _gateprep_kernel(g_ref, b_ref, p_ref, *, nc, chunk):
    s = nc * chunk
    gt = jnp.concatenate([g_ref[0, c] + b_ref[...] for c in range(nc)], axis=1)
    ig = gt[0:8]
    fg = gt[8:16]
    row = lax.broadcasted_iota(jnp.int32, (8, s), 0)
    pos = lax.broadcasted_iota(jnp.int32, (8, s), 1) & (chunk - 1)
    fwd = row < M_HEADS

    def seg_scan(x, op, ident):
        k = 1
        while k < chunk:
            sh_f = jnp.where(pos >= k, pltpu.roll(x, k, axis=1), ident)
            sh_b = jnp.where(pos < chunk - k, pltpu.roll(x, s - k, axis=1), ident)
            x = op(x, jnp.where(fwd, sh_f, sh_b))
            k *= 2
        return x

    lf = jnp.minimum(fg, 0.0) - jnp.log1p(jnp.exp(-jnp.abs(fg)))
    bsum = seg_scan(lf, jnp.add, 0.0)
    beta = ig - bsum
    cmax = seg_scan(beta, jnp.maximum, -jnp.inf)

    fwd1 = fwd[:, :1]
    g_max = [jnp.max(beta[:, c * chunk:(c + 1) * chunk], axis=1, keepdims=True) for c in range(nc)]
    b_last = [jnp.where(fwd1, bsum[:, (c + 1) * chunk - 1:(c + 1) * chunk], bsum[:, c * chunk:c * chunk + 1])
              for c in range(nc)]
    m_f = [None] * nc
    m_b = [None] * nc
    mf = jnp.zeros((8, 1), F32)
    mb = jnp.zeros((8, 1), F32)
    for i in range(nc):
        j = nc - 1 - i
        m_f[i] = mf
        m_b[j] = mb
        mf = b_last[i] + jnp.maximum(mf, g_max[i])
        mb = b_last[j] + jnp.maximum(mb, g_max[j])
    wide = lambda v: jnp.broadcast_to(v, (8, chunk))
    m_in = jnp.concatenate([wide(jnp.where(fwd1, m_f[c], m_b[c])) for c in range(nc)], axis=1)
    g_all = jnp.concatenate([wide(g_max[c]) for c in range(nc)], axis=1)

    mx = jnp.maximum(cmax, m_in)
    m_c = jnp.maximum(m_in, g_all)
    quantities = (beta * LOG2E, mx * LOG2E, jnp.exp(m_in - mx), jnp.exp(-(bsum + mx)), jnp.exp(beta - m_c),
                  jnp.exp(m_in - m_c))
    row16 = lax.broadcasted_iota(jnp.int32, (16, s), 0)
    for h in range(M_HEADS):
        tile = jnp.zeros((16, s), F32)
        for d in range(2):
            for qi, val in enumerate(quantities):
                src = val[d * M_HEADS + h:d * M_HEADS + h + 1]
                tile = jnp.where(row16 == d * 8 + qi, jnp.broadcast_to(src, (16, s)), tile)
        for c in range(nc):
            p_ref[0, c, h] = tile[:, c * chunk:(c + 1) * chunk]


def _gateprep(gates, bias, chunk):
    bsz, nc, ng, _ = gates.shape
    return pl.pallas_call(
        functools.partial(_gateprep_kernel, nc=nc, chunk=chunk),
        grid=(bsz,),
        in_specs=[pl.BlockSpec((1, nc, ng, chunk), lambda b: (b, 0, 0, 0)),
                  _resident((ng, chunk))],
        out_specs=pl.BlockSpec((1, nc, M_HEADS, 16, chunk), lambda b: (b, 0, 0, 0, 0)),
        out_shape=jax.ShapeDtypeStruct((bsz, nc, M_HEADS, 16, chunk), F32),
        compiler_params=_cparams(1),
        name="gateprep",
    )(gates, bias)


def _mlstm_kernel(qt_ref, vt_ref, ot_ref, k_ref, p_ref, hg_ref, at_ref, acc, ct_scr, n_scr,
                  *, nc, chunk):
    L = chunk
    ct_scr[...] = jnp.zeros_like(ct_scr)
    n_scr[...] = jnp.zeros_like(n_scr)
    s_idx = lax.broadcasted_iota(jnp.int32, (L, L), 0)
    t_idx = lax.broadcasted_iota(jnp.int32, (L, L), 1)
    tri = (s_idx <= t_idx, s_idx >= t_idx)
    n_pad = 16

    def body(i, carry, *, finish):
        cs = (i, nc - 1 - i)
        qt = [qt_ref[0, c] for c in cs]
        vt = [vt_ref[0, c] for c in cs]
        k = [k_ref[0, pl.ds(pl.multiple_of(c * L, L), L), :] for c in cs]
        prm = [p_ref[0, cs[d], 0, d * 8:(d + 1) * 8, :] for d in range(2)]
        ct_old = [ct_scr[d] for d in range(2)]
        n_old = [n_scr[d] for d in range(2)]
        r1 = []
        for d in range(2):
            lhs = jnp.concatenate([k[d], ct_old[d].astype(BF16),
                                   jnp.concatenate([n_old[d], n_old[d]], axis=0).astype(BF16)], axis=0)
            r1.append(jnp.dot(lhs, qt[d], preferred_element_type=F32))
        r2 = []
        for d in range(2):
            w = prm[d][4:5]
            vts = (vt[d].astype(F32) * w).astype(BF16)
            lhs2 = jnp.concatenate([vts, jnp.broadcast_to(w, (n_pad, L)).astype(BF16)], axis=0)
            r2.append(jnp.dot(lhs2, k[d], preferred_element_type=F32))
        for d in range(2):
            beta, mx, w_inter, neg_mt, _, decay = (prm[d][j:j + 1] for j in range(N_PREP))
            d_t = jnp.where(tri[d], jnp.exp2(jnp.broadcast_to(beta, (L, L)).T - mx), 0.0)
            p_t = r1[d][0:L] * d_t
            inter = r1[d][L:L + M_DV]
            qn = r1[d][L + M_DV:L + M_DV + 1]
            num = jnp.dot(vt[d], p_t.astype(BF16), preferred_element_type=F32) + w_inter * inter
            den = jnp.sum(p_t, axis=0, keepdims=True) + w_inter * qn
            hout = num * (1.0 / jnp.maximum(jnp.abs(den), neg_mt))
            if finish:
                hs = acc[cs[d]] + hout
                ms = jnp.mean(hs * hs, axis=0, keepdims=True)
                y = hs * lax.rsqrt(ms + EPS) * hg_ref[0]
                gate = jax.nn.sigmoid(ot_ref[0, cs[d]].astype(F32))
                at_ref[0, cs[d]] = (y * gate).astype(BF16)
            else:
                acc[cs[d]] = hout
            ct_scr[d] = decay * ct_old[d] + r2[d][0:M_DV]
            n_scr[d] = decay * n_old[d] + r2[d][M_DV:M_DV + 8]
        return carry

    half = nc // 2
    lax.fori_loop(0, half, functools.partial(body, finish=False), 0, unroll=2)
    lax.fori_loop(half, nc, functools.partial(body, finish=True), 0, unroll=2)


def _mlstm_scan(k_tok, qvo_t, prep, head_g, chunk):
    bsz, s, _ = k_tok.shape
    nc = s // chunk
    assert nc % 4 == 0
    blk_t = lambda off: pl.BlockSpec((1, nc, M_DK, chunk), lambda b, h: (b, 0, off + h, 0))
    return pl.pallas_call(
        functools.partial(_mlstm_kernel, nc=nc, chunk=chunk),
        grid=(bsz, M_HEADS),
        in_specs=[blk_t(0), blk_t(M_HEADS), blk_t(2 * M_HEADS),
                  pl.BlockSpec((1, s, M_DK), lambda b, h: (b, 0, h)),
                  pl.BlockSpec((1, nc, 1, 16, chunk), lambda b, h: (b, 0, h, 0, 0)),
                  pl.BlockSpec((1, M_DV, 1), lambda b, h: (h, 0, 0))],
        out_specs=pl.BlockSpec((1, nc, M_DV, chunk), lambda b, h: (b, 0, h, 0)),
        out_shape=jax.ShapeDtypeStruct((bsz, nc, M_HEADS * M_DV, chunk), BF16),
        scratch_shapes=[pltpu.VMEM((nc, M_DV, chunk), F32),
                        pltpu.VMEM((2, M_DV, M_DK), F32),
                        pltpu.VMEM((2, 8, M_DK), F32)],
        compiler_params=_cparams(2),
        name="mlstm_scan",
    )(qvo_t, qvo_t, qvo_t, k_tok, prep, head_g.reshape(M_HEADS, M_DV, 1))


def _mix_ffn_kernel(a_ref, ap_ref, an_ref, x_ref, xp_ref, xn_ref, wo_ref, g1_ref, g_ref, sc_ref, sh_ref,
                    gate_ref, wup_ref, cw_ref, cb_ref, wdn_ref, fg_ref, o_ref, h_scr, a_scr,
                    *, tm, final):
    i = pl.program_id(1)
    last = pl.num_programs(1) - 1
    wo = wo_ref[...]
    g1 = g1_ref[0, 0]

    def mix(a_t):
        return lax.dot_general(a_t, wo, (((0,), (0,)), ((), ())), preferred_element_type=F32)

    a_main = jnp.concatenate([a_ref[0, cc] for cc in range(a_ref.shape[1])], axis=1)
    x1 = x_ref[0] + g1 * mix(a_main)
    x1p = xp_ref[0] + g1 * mix(ap_ref[0, 0][:, LANES - HALO:])
    x1n = xn_ref[0] + g1 * mix(an_ref[0, 0][:, :HALO])

    g = g_ref[...]
    sc = sc_ref[0, 0]
    sh = sh_ref[0, 0]
    hp = jnp.where(i > 0, _norm_mod(x1p, g, sc, sh), 0.0)
    hn = jnp.where(i < last, _norm_mod(x1n, g, sc, sh), 0.0)
    h_scr[0:HALO] = hp.astype(BF16)
    h_scr[HALO:HALO + tm] = _norm_mod(x1, g, sc, sh).astype(BF16)
    h_scr[HALO + tm:] = hn.astype(BF16)
    h = h_scr[...]
    ext = tm + 2 * HALO

    def conv(u, off):
        w0 = cw_ref[0, 0:1, off:off + FFN_CHUNK]
        w1 = cw_ref[0, 1:2, off:off + FFN_CHUNK]
        w2 = cw_ref[0, 2:3, off:off + FFN_CHUNK]
        prev = pltpu.roll(u, 1, axis=0)
        nxt = pltpu.roll(u, ext - 1, axis=0)
        out = w0 * prev + w1 * u + w2 * nxt + cb_ref[0, :, off:off + FFN_CHUNK]
        return out[HALO:HALO + tm]

    for f in range(D_FF // FFN_CHUNK):
        og = f * FFN_CHUNK
        ov = D_FF + f * FFN_CHUNK
        ug = conv(jnp.dot(h, wup_ref[0, :, og:og + FFN_CHUNK], preferred_element_type=F32), og)
        uv = conv(jnp.dot(h, wup_ref[0, :, ov:ov + FFN_CHUNK], preferred_element_type=F32), ov)
        a_scr[:, og:og + FFN_CHUNK] = (ug * jax.nn.sigmoid(ug) * uv).astype(BF16)

    y = jnp.dot(a_scr[...], wdn_ref[0], preferred_element_type=F32)
    out = x1 + gate_ref[0, 0] * y
    if final:
        ms = jnp.mean(out * out, axis=-1, keepdims=True)
        out = out * lax.rsqrt(ms + EPS) * fg_ref[...]
    o_ref[0] = out


def _mix_ffn(a_t, w_o, x, g, mod4, layer, w_up, conv_w, conv_b, w_down, final_g, *, tm, chunk, final):
    bsz, s, d = x.shape
    kdim = w_o.shape[0]
    nc = s // chunk
    cpt = tm // chunk
    hb = tm // HALO
    nh = s // HALO
    return pl.pallas_call(
        functools.partial(_mix_ffn_kernel, tm=tm, final=final),
        grid=(bsz, s // tm),
        in_specs=[pl.BlockSpec((1, cpt, kdim, chunk), lambda b, i: (b, i, 0, 0)),
                  pl.BlockSpec((1, 1, kdim, LANES),
                               lambda b, i: (b, jnp.maximum(i * cpt - 1, 0), 0, chunk // LANES - 1)),
                  pl.BlockSpec((1, 1, kdim, LANES),
                               lambda b, i: (b, jnp.minimum((i + 1) * cpt, nc - 1), 0, 0)),
                  pl.BlockSpec((1, tm, d), lambda b, i: (b, i, 0)),
                  pl.BlockSpec((1, HALO, d), lambda b, i: (b, jnp.maximum(i * hb - 1, 0), 0)),
                  pl.BlockSpec((1, HALO, d), lambda b, i: (b, jnp.minimum((i + 1) * hb, nh - 1), 0)),
                  _resident((kdim, d)),
                  pl.BlockSpec((1, 1, 1, d), lambda b, i: (b, 2, 0, 0)),
                  _resident((1, d)),
                  pl.BlockSpec((1, 1, 1, d), lambda b, i: (b, 4, 0, 0)),
                  pl.BlockSpec((1, 1, 1, d), lambda b, i: (b, 3, 0, 0)),
                  pl.BlockSpec((1, 1, 1, d), lambda b, i: (b, 5, 0, 0)),
                  _resident((d, 2 * D_FF), layer),
                  _resident((3, 2 * D_FF), layer),
                  _resident((1, 2 * D_FF), layer),
                  _resident((D_FF, d), layer),
                  _resident((1, d))],
        out_specs=pl.BlockSpec((1, tm, d), lambda b, i: (b, i, 0)),
        out_shape=jax.ShapeDtypeStruct((bsz, s, d), F32),
        scratch_shapes=[pltpu.VMEM((tm + 2 * HALO, d), BF16),
                        pltpu.VMEM((tm, D_FF), BF16)],
        compiler_params=_cparams(2),
        name="mix_ffn_final" if final else "mix_ffn",
    )(a_t, a_t, a_t, x, x, x, w_o, mod4, g.reshape(1, d), mod4, mod4, mod4, w_up, conv_w,
      conv_b.reshape(DEPTH, 1, -1), w_down, final_g.reshape(1, d))


def _t5_bucket(rel):
    nb = NUM_BUCKETS // 2
    max_exact = nb // 2
    ret = jnp.where(rel > 0, nb, 0)
    n = jnp.abs(rel)
    nf = jnp.maximum(n, 1).astype(jnp.float32)
    large = max_exact + (jnp.log(nf / max_exact) / math.log(MAX_DISTANCE / max_exact)
                         * (nb - max_exact)).astype(jnp.int32)
    large = jnp.minimum(large, nb - 1)
    return ret + jnp.where(n < max_exact, n, large)


N_EDGE = 3


def _bias_kernel(bucket_ref, rb_ref, o_ref):
    var = pl.program_id(0)
    bucket = bucket_ref[...]
    c = lax.broadcasted_iota(jnp.int32, bucket.shape, 0)
    q = lax.broadcasted_iota(jnp.int32, bucket.shape, 1)
    ok = jnp.abs(c - WINDOW - q) <= WINDOW
    ok = ok & ((c >= WINDOW) | (var != 0)) & ((c < WINDOW + A_BLOCK) | (var != N_EDGE - 1))
    hit = [bucket == kb for kb in range(NUM_BUCKETS)]
    for hh in range(A_HEADS):
        head = (hh % A_KV) * A_GROUP + hh // A_KV
        val = jnp.zeros(bucket.shape, F32)
        for kb in range(NUM_BUCKETS):
            val = jnp.where(hit[kb], rb_ref[kb, head], val)
        o_ref[0, hh] = jnp.where(ok, val * LOG2E, -jnp.inf)


def _bias_table(rel_bias):
    c_idx = jnp.arange(A_SPAN)[:, None]
    q_idx = jnp.arange(A_BLOCK)[None, :]
    bucket = _t5_bucket(c_idx - WINDOW - q_idx).astype(jnp.int32)
    return pl.pallas_call(
        _bias_kernel,
        grid=(N_EDGE,),
        in_specs=[pl.BlockSpec((A_SPAN, A_BLOCK), lambda v: (0, 0)),
                  pl.BlockSpec(memory_space=pltpu.SMEM)],
        out_specs=pl.BlockSpec((1, A_HEADS, A_SPAN, A_BLOCK), lambda v: (v, 0, 0, 0)),
        out_shape=jax.ShapeDtypeStruct((N_EDGE, A_HEADS, A_SPAN, A_BLOCK), F32),
        compiler_params=_cparams(1),
        name="bias_table",
    )(bucket, rel_bias)


def _swa_kernel(qt_ref, kp_ref, kc_ref, kn_ref, vp_ref, vc_ref, vn_ref, bias_ref, sink_ref, o_ref):
    j = pl.program_id(1)
    last = pl.num_programs(1) - 1
    n_sub = SWA_SUB
    kblk = [kp_ref[0]] + [kc_ref[0, u * A_BLOCK:(u + 1) * A_BLOCK] for u in range(n_sub)] + [kn_ref[0]]
    vblk = [vp_ref[0, 0]] + [vc_ref[0, u] for u in range(n_sub)] + [vn_ref[0, 0]]
    var = [1] * n_sub
    var[0] = jnp.where(j == 0, 0, 1)
    var[-1] = jnp.where(j == last, N_EDGE - 1, 1)
    lane_kv = lax.broadcasted_iota(jnp.int32, (A_SPAN, A_KV * A_HD), 1) // A_HD
    row_kv = lax.broadcasted_iota(jnp.int32, (A_KV * A_HD, A_SPAN), 0) // A_HD
    grp = lax.broadcasted_iota(jnp.int32, (1, A_GROUP * A_BLOCK), 1) // A_BLOCK
    groups = range(A_GROUP)
    kcat = []
    vstack = []
    qp = []
    for u in range(n_sub):
        kcat.append(jnp.concatenate(kblk[u:u + 3], axis=0))
        vcat = jnp.concatenate(vblk[u:u + 3], axis=1)
        vstack.append(jnp.concatenate([jnp.where(row_kv == kv, vcat, jnp.zeros_like(vcat))
                                       for kv in range(A_KV)], axis=1))
        qp.append(jnp.concatenate([qt_ref[0, u, g * MXU_DIM:(g + 1) * MXU_DIM, :] for g in groups],
                                  axis=1))
    ps = [[] for _ in range(n_sub)]
    rden = [[] for _ in range(n_sub)]
    for kv in range(A_KV):
        sink = jnp.zeros((1, A_GROUP * A_BLOCK), F32)
        for g in groups:
            sink = jnp.where(grp == g, sink_ref[kv * A_GROUP + g], sink)
        sink = sink * LOG2E
        for u in range(n_sub):
            bias = jnp.concatenate([bias_ref[var[u], g * A_KV + kv] for g in groups], axis=1)
            k_kv = jnp.where(lane_kv == kv, kcat[u], jnp.zeros_like(kcat[u]))
            s = jnp.dot(k_kv, qp[u], preferred_element_type=F32) + bias
            m = jnp.maximum(jnp.max(s, axis=0, keepdims=True), sink)
            p = jnp.exp2(s - m)
            rden[u].append(1.0 / (jnp.sum(p, axis=0, keepdims=True) + jnp.exp2(sink - m)))
            ps[u].append(p.astype(BF16))
    for u in range(n_sub):
        out = jnp.dot(vstack[u], jnp.concatenate(ps[u], axis=0), preferred_element_type=F32)
        out = jnp.concatenate([out[kv * A_HD:(kv + 1) * A_HD] * rden[u][kv] for kv in range(A_KV)], axis=0)
        for g in groups:
            o_ref[0, u, g * MXU_DIM:(g + 1) * MXU_DIM, :] = (
                out[:, g * A_BLOCK:(g + 1) * A_BLOCK].astype(BF16))


def _swa(k_tok, qv_t, bias_tbl, sink):
    bsz, s, _ = k_tok.shape
    nb = s // A_BLOCK
    ns = SWA_SUB
    assert nb % ns == 0 and nb >= 2 * ns
    qw = A_HEADS * A_HD
    vrow = qw // MXU_DIM
    prev = lambda j: jnp.maximum(ns * j - 1, 0)
    nxt = lambda j: jnp.minimum(ns * j + ns, nb - 1)
    k_edge = lambda row: pl.BlockSpec((1, A_BLOCK, MXU_DIM), lambda b, j: (b, row(j), 0))
    v_edge = lambda row: pl.BlockSpec((1, 1, MXU_DIM, A_BLOCK), lambda b, j: (b, row(j), vrow, 0))
    return pl.pallas_call(
        _swa_kernel,
        grid=(bsz, nb // ns),
        in_specs=[pl.BlockSpec((1, ns, qw, A_BLOCK), lambda b, j: (b, j, 0, 0)),
                  k_edge(prev), pl.BlockSpec((1, ns * A_BLOCK, MXU_DIM), lambda b, j: (b, j, 0)), k_edge(nxt),
                  v_edge(prev), pl.BlockSpec((1, ns, MXU_DIM, A_BLOCK), lambda b, j: (b, j, vrow, 0)),
                  v_edge(nxt),
                  _resident((N_EDGE, A_HEADS, A_SPAN, A_BLOCK)),
                  pl.BlockSpec(memory_space=pltpu.SMEM)],
        out_specs=pl.BlockSpec((1, ns, qw, A_BLOCK), lambda b, j: (b, j, 0, 0)),
        out_shape=jax.ShapeDtypeStruct((bsz, nb, qw, A_BLOCK), BF16),
        compiler_params=_cparams(2),
        name="swa",
    )(qv_t, k_tok, k_tok, k_tok, qv_t, qv_t, qv_t, bias_tbl, sink)


def _prep_weights(mlstm_w_in, mlstm_b_gate, mlstm_w_out, attn_w_in, attn_w_out, ffn_w_up, ffn_w_down):
    qk = M_HEADS * M_DK
    w_m = mlstm_w_in[0]
    w_q = w_m[:, :qk] * (M_DK ** -0.5)
    perm = lambda g: g.reshape(-1, 2, 2, M_HEADS).swapaxes(1, 2).reshape(-1, N_GATES)
    w_t = jnp.concatenate([w_q, w_m[:, 2 * qk:4 * qk], perm(w_m[:, 4 * qk:])], axis=1).T.astype(BF16)
    w_k = w_m[:, qk:2 * qk].astype(BF16)
    b_gate = jnp.broadcast_to(perm(mlstm_b_gate[0][None]).T, (N_GATES, SCAN_CHUNK))
    qw = A_HEADS * A_HD
    kw = A_KV * A_HD
    w_a = attn_w_in[0]
    wq = w_a[:, :qw].reshape(D_MODEL, A_KV, A_GROUP, A_HD).transpose(0, 2, 1, 3).reshape(D_MODEL, qw)
    w_attn_k = w_a[:, qw:qw + kw].astype(BF16)
    w_attn_t = jnp.concatenate([wq * (A_HD ** -0.5 * LOG2E), w_a[:, qw + kw:]], axis=1).T.astype(BF16)
    wo_attn = attn_w_out[0].reshape(A_KV, A_GROUP, A_HD, D_MODEL).transpose(1, 0, 2, 3)
    wo_attn = wo_attn.reshape(qw, D_MODEL).astype(BF16)
    return dict(w_k=w_k, w_t=w_t, b_gate=b_gate,
                wo_m=mlstm_w_out[0].astype(BF16), w_attn_k=w_attn_k, w_attn_t=w_attn_t, wo_attn=wo_attn,
                w_up=ffn_w_up.astype(BF16), w_down=ffn_w_down.astype(BF16))


def _trunk(x, mod, wts, bias_tbl, norm_g, mlstm_head_g, attn_sink, ffn_conv_w, ffn_conv_b, final_g,
           *, tm, tm_ffn):
    bsz, s, d = x.shape
    mod4 = mod[0].reshape(bsz, 6, 1, d)
    k_tok, qvo_t, gates = _inproj_split(x, norm_g[0, 0], mod4, wts["w_k"], wts["w_t"],
                                        tm=tm, chunk=SCAN_CHUNK, tn=512, n_extra=N_GATES)
    prep = _gateprep(gates, wts["b_gate"], SCAN_CHUNK)
    a_t = _mlstm_scan(k_tok, qvo_t, prep, mlstm_head_g[0], SCAN_CHUNK)
    x = _mix_ffn(a_t, wts["wo_m"], x, norm_g[0, 1], mod4, 0, wts["w_up"], ffn_conv_w, ffn_conv_b,
                 wts["w_down"], final_g, tm=tm_ffn, chunk=SCAN_CHUNK, final=False)
    mod4 = mod[1].reshape(bsz, 6, 1, d)
    k_tok, qv_t = _inproj_split(x, norm_g[1, 0], mod4, wts["w_attn_k"], wts["w_attn_t"],
                                tm=tm, chunk=A_BLOCK, tn=256)
    a_t = _swa(k_tok, qv_t, bias_tbl, attn_sink[0])
    return _mix_ffn(a_t, wts["wo_attn"], x, norm_g[1, 1], mod4, 1, wts["w_up"], ffn_conv_w,
                    ffn_conv_b, wts["w_down"], final_g, tm=tm_ffn, chunk=A_BLOCK, final=True)


def kernel(x_prompt, x_sample, c_prompt, c_sample, adaln_w, adaln_b, norm_g, mlstm_w_in, mlstm_b_gate, mlstm_head_g, mlstm_w_out, attn_w_in, attn_sink, attn_w_out, rel_bias, ffn_w_up, ffn_conv_w, ffn_conv_b, ffn_w_down, final_g):
    nbp = x_prompt.shape[0]
    mod = _adaln(jnp.concatenate([c_prompt, c_sample], axis=0), adaln_w, adaln_b)
    wts = _prep_weights(mlstm_w_in, mlstm_b_gate, mlstm_w_out, attn_w_in, attn_w_out, ffn_w_up, ffn_w_down)
    bias_tbl = _bias_table(rel_bias)
    run = functools.partial(_trunk, wts=wts, bias_tbl=bias_tbl, norm_g=norm_g,
                            mlstm_head_g=mlstm_head_g, attn_sink=attn_sink, ffn_conv_w=ffn_conv_w,
                            ffn_conv_b=ffn_conv_b, final_g=final_g, tm=512, tm_ffn=1024)
    return run(x_prompt, mod[:, :nbp]), run(x_sample, mod[:, nbp:])
```

```python
import functools
import math

import jax
import jax.numpy as jnp
from jax import lax
from jax.experimental import pallas as pl
from jax.experimental.pallas import tpu as pltpu

F32 = jnp.float32
BF16 = jnp.bfloat16

D_MODEL = 1024
DEPTH = 2
M_HEADS = 4
M_DK = 256
M_DV = 256
A_HEADS = 16
A_KV = 4
A_GROUP = A_HEADS // A_KV
A_HD = 64
WINDOW = 128
A_BLOCK = 128
A_SPAN = A_BLOCK + 2 * WINDOW
NUM_BUCKETS = 32
MAX_DISTANCE = 128
D_FF = 2816
EPS = 1e-6
LOG2E = 1.4426950408889634

LANES = 128
MXU_DIM = 256
VMEM_LIMIT = 56 * 1024 * 1024

SCAN_CHUNK = 256
N_GATES = 4 * M_HEADS
HALO = 16
FFN_CHUNK = MXU_DIM
SWA_SUB = 8


def _cparams(n_axes):
    return pltpu.CompilerParams(dimension_semantics=("parallel",) * n_axes,
                                vmem_limit_bytes=VMEM_LIMIT)


def _resident(shape, layer=None):
    nd = len(shape)
    if layer is None:
        return pl.BlockSpec(shape, lambda *_: (0,) * nd, pipeline_mode=pl.Buffered(1))
    return pl.BlockSpec((1,) + tuple(shape), lambda *_: (layer,) + (0,) * nd,
                        pipeline_mode=pl.Buffered(1))


def _norm_mod(x, g, sc, sh):
    ms = jnp.mean(x * x, axis=-1, keepdims=True)
    return x * lax.rsqrt(ms + EPS) * (g * (1.0 + sc)) + sh


def _adaln_kernel(c_ref, w_ref, b_ref, o_ref):
    c = c_ref[...]
    s = c * jax.nn.sigmoid(c)
    o_ref[0] = jnp.dot(s, w_ref[0], precision=lax.Precision.HIGHEST,
                       preferred_element_type=F32) + b_ref[0]


def _adaln(c_all, adaln_w, adaln_b):
    nb = c_all.shape[0]
    n = adaln_w.shape[-1]
    tn = 1536
    return pl.pallas_call(
        _adaln_kernel,
        grid=(DEPTH, n // tn),
        in_specs=[pl.BlockSpec((nb, D_MODEL), lambda l, j: (0, 0)),
                  pl.BlockSpec((1, D_MODEL, tn), lambda l, j: (l, 0, j)),
                  pl.BlockSpec((1, 1, tn), lambda l, j: (l, 0, j))],
        out_specs=pl.BlockSpec((1, nb, tn), lambda l, j: (l, 0, j)),
        out_shape=jax.ShapeDtypeStruct((DEPTH, nb, n), F32),
        compiler_params=_cparams(2),
        name="adaln",
    )(c_all, adaln_w, adaln_b.reshape(DEPTH, 1, n))


def _inproj_split_kernel(x_ref, g_ref, sc_ref, sh_ref, wk_ref, wt_ref, k_ref, t_ref, *rest,
                         tn, chunk, n_extra):
    h = _norm_mod(x_ref[0], g_ref[...], sc_ref[0, 0], sh_ref[0, 0]).astype(BF16)
    tm = h.shape[0]
    nk = wk_ref.shape[1]
    tk = min(tn, nk)
    for j in range(nk // tk):
        k_ref[0, :, j * tk:(j + 1) * tk] = jnp.dot(
            h, wk_ref[:, j * tk:(j + 1) * tk], preferred_element_type=F32).astype(BF16)
    nt = wt_ref.shape[0] - n_extra
    for j in range(nt // tn):
        hi = (j + 1) * tn + (n_extra if j == nt // tn - 1 else 0)
        res = lax.dot_general(wt_ref[j * tn:hi, :], h, (((1,), (1,)), ((), ())),
                              preferred_element_type=F32)
        for cc in range(tm // chunk):
            cols = slice(cc * chunk, (cc + 1) * chunk)
            t_ref[0, cc, j * tn:(j + 1) * tn, :] = res[:tn, cols].astype(BF16)
            if hi > (j + 1) * tn:
                rest[0][0, cc] = res[tn:, cols]


def _inproj_split(x, g, mod4, w_k, w_t, *, tm, chunk, tn, n_extra=0):
    bsz, s, d = x.shape
    nk = w_k.shape[1]
    nt = w_t.shape[0] - n_extra
    out_specs = [pl.BlockSpec((1, tm, nk), lambda b, i: (b, i, 0)),
                 pl.BlockSpec((1, tm // chunk, nt, chunk), lambda b, i: (b, i, 0, 0))]
    out_shape = [jax.ShapeDtypeStruct((bsz, s, nk), BF16),
                 jax.ShapeDtypeStruct((bsz, s // chunk, nt, chunk), BF16)]
    if n_extra:
        out_specs.append(pl.BlockSpec((1, tm // chunk, n_extra, chunk), lambda b, i: (b, i, 0, 0)))
        out_shape.append(jax.ShapeDtypeStruct((bsz, s // chunk, n_extra, chunk), F32))
    return pl.pallas_call(
        functools.partial(_inproj_split_kernel, tn=tn, chunk=chunk, n_extra=n_extra),
        grid=(bsz, s // tm),
        in_specs=[pl.BlockSpec((1, tm, d), lambda b, i: (b, i, 0)),
                  _resident((1, d)),
                  pl.BlockSpec((1, 1, 1, d), lambda b, i: (b, 1, 0, 0)),
                  pl.BlockSpec((1, 1, 1, d), lambda b, i: (b, 0, 0, 0)),
                  _resident((d, nk)), _resident((nt + n_extra, d))],
        out_specs=out_specs, out_shape=out_shape,
        compiler_params=_cparams(2),
        name="inproj_m" if n_extra else "inproj_a",
    )(x, g.reshape(1, d), mod4, mod4, w_k, w_t)


N_PREP = 6


def _gateprep_kernel(g_ref, b_ref, p_ref, *, nc, chunk):
    s = nc * chunk
    gt = jnp.concatenate([g_ref[0, c] + b_ref[...] for c in range(nc)], axis=1)
    ig = gt[0:8]
    fg = gt[8:16]
    row = lax.broadcasted_iota(jnp.int32, (8, s), 0)
    pos = lax.broadcasted_iota(jnp.int32, (8, s), 1) & (chunk - 1)
    fwd = row < M_HEADS

    def seg_scan(x, op, ident):
        k = 1
        while k < chunk:
            sh_f = jnp.where(pos >= k, pltpu.roll(x, k, axis=1), ident)
            sh_b = jnp.where(pos < chunk - k, pltpu.roll(x, s - k, axis=1), ident)
            x = op(x, jnp.where(fwd, sh_f, sh_b))
            k *= 2
        return x

    lf = jnp.minimum(fg, 0.0) - jnp.log1p(jnp.exp(-jnp.abs(fg)))
    bsum = seg_scan(lf, jnp.add, 0.0)
    beta = ig - bsum
    cmax = seg_scan(beta, jnp.maximum, -jnp.inf)

    fwd1 = fwd[:, :1]
    g_max = [jnp.max(beta[:, c * chunk:(c + 1) * chunk], axis=1, keepdims=True) for c in range(nc)]
    b_last = [jnp.where(fwd1, bsum[:, (c + 1) * chunk - 1:(c + 1) * chunk], bsum[:, c * chunk:c * chunk + 1])
              for c in range(nc)]
    m_f = [None] * nc
    m_b = [None] * nc
    mf = jnp.zeros((8, 1), F32)
    mb = jnp.zeros((8, 1), F32)
    for i in range(nc):
        j = nc - 1 - i
        m_f[i] = mf
        m_b[j] = mb
        mf = b_last[i] + jnp.maximum(mf, g_max[i])
        mb = b_last[j] + jnp.maximum(mb, g_max[j])
    wide = lambda v: jnp.broadcast_to(v, (8, chunk))
    m_in = jnp.concatenate([wide(jnp.where(fwd1, m_f[c], m_b[c])) for c in range(nc)], axis=1)
    g_all = jnp.concatenate([wide(g_max[c]) for c in range(nc)], axis=1)

    mx = jnp.maximum(cmax, m_in)
    m_c = jnp.maximum(m_in, g_all)
    quantities = (beta * LOG2E, mx * LOG2E, jnp.exp(m_in - mx), jnp.exp(-(bsum + mx)), jnp.exp(beta - m_c),
                  jnp.exp(m_in - m_c))
    row16 = lax.broadcasted_iota(jnp.int32, (16, s), 0)
    for h in range(M_HEADS):
        tile = jnp.zeros((16, s), F32)
        for d in range(2):
            for qi, val in enumerate(quantities):
                src = val[d * M_HEADS + h:d * M_HEADS + h + 1]
                tile = jnp.where(row16 == d * 8 + qi, jnp.broadcast_to(src, (16, s)), tile)
        for c in range(nc):
            p_ref[0, c, h] = tile[:, c * chunk:(c + 1) * chunk]


def _gateprep(gates, bias, chunk):
    bsz, nc, ng, _ = gates.shape
    return pl.pallas_call(
        functools.partial(_gateprep_kernel, nc=nc, chunk=chunk),
        grid=(bsz,),
        in_specs=[pl.BlockSpec((1, nc, ng, chunk), lambda b: (b, 0, 0, 0)),
                  _resident((ng, chunk))],
        out_specs=pl.BlockSpec((1, nc, M_HEADS, 16, chunk), lambda b: (b, 0, 0, 0, 0)),
        out_shape=jax.ShapeDtypeStruct((bsz, nc, M_HEADS, 16, chunk), F32),
        compiler_params=_cparams(1),
        name="gateprep",
    )(gates, bias)


def _mlstm_kernel(qt_ref, vt_ref, ot_ref, k_ref, p_ref, hg_ref, at_ref, acc, ct_scr, n_scr,
                  *, nc, chunk):
    L = chunk
    ct_scr[...] = jnp.zeros_like(ct_scr)
    n_scr[...] = jnp.zeros_like(n_scr)
    s_idx = lax.broadcasted_iota(jnp.int32, (L, L), 0)
    t_idx = lax.broadcasted_iota(jnp.int32, (L, L), 1)
    tri = (s_idx <= t_idx, s_idx >= t_idx)
    n_pad = 16

    def body(i, carry, *, finish):
        cs = (i, nc - 1 - i)
        qt = [qt_ref[0, c] for c in cs]
        vt = [vt_ref[0, c] for c in cs]
        k = [k_ref[0, pl.ds(pl.multiple_of(c * L, L), L), :] for c in cs]
        prm = [p_ref[0, cs[d], 0, d * 8:(d + 1) * 8, :] for d in range(2)]
        ct_old = [ct_scr[d] for d in range(2)]
        n_old = [n_scr[d] for d in range(2)]
        r1 = []
        for d in range(2):
            lhs = jnp.concatenate([k[d], ct_old[d].astype(BF16),
                                   jnp.concatenate([n_old[d], n_old[d]], axis=0).astype(BF16)], axis=0)
            r1.append(jnp.dot(lhs, qt[d], preferred_element_type=F32))
        r2 = []
        for d in range(2):
            w = prm[d][4:5]
            vts = (vt[d].astype(F32) * w).astype(BF16)
            lhs2 = jnp.concatenate([vts, jnp.broadcast_to(w, (n_pad, L)).astype(BF16)], axis=0)
            r2.append(jnp.dot(lhs2, k[d], preferred_element_type=F32))
        for d in range(2):
            beta, mx, w_inter, neg_mt, _, decay = (prm[d][j:j + 1] for j in range(N_PREP))
            d_t = jnp.where(tri[d], jnp.exp2(jnp.broadcast_to(beta, (L, L)).T - mx), 0.0)
            p_t = r1[d][0:L] * d_t
            inter = r1[d][L:L + M_DV]
            qn = r1[d][L + M_DV:L + M_DV + 1]
            num = jnp.dot(vt[d], p_t.astype(BF16), preferred_element_type=F32) + w_inter * inter
            den = jnp.sum(p_t, axis=0, keepdims=True) + w_inter * qn
            hout = num * (1.0 / jnp.maximum(jnp.abs(den), neg_mt))
            if finish:
                hs = acc[cs[d]] + hout
                ms = jnp.mean(hs * hs, axis=0, keepdims=True)
                y = hs * lax.rsqrt(ms + EPS) * hg_ref[0]
                gate = jax.nn.sigmoid(ot_ref[0, cs[d]].astype(F32))
                at_ref[0, cs[d]] = (y * gate).astype(BF16)
            else:
                acc[cs[d]] = hout
            ct_scr[d] = decay * ct_old[d] + r2[d][0:M_DV]
            n_scr[d] = decay * n_old[d] + r2[d][M_DV:M_DV + 8]
        return carry

    half = nc // 2
    lax.fori_loop(0, half, functools.partial(body, finish=False), 0, unroll=2)
    lax.fori_loop(half, nc, functools.partial(body, finish=True), 0, unroll=2)


def _mlstm_scan(k_tok, qvo_t, prep, head_g, chunk):
    bsz, s, _ = k_tok.shape
    nc = s // chunk
    assert nc % 4 == 0
    blk_t = lambda off: pl.BlockSpec((1, nc, M_DK, chunk), lambda b, h: (b, 0, off + h, 0))
    return pl.pallas_call(
        functools.partial(_mlstm_kernel, nc=nc, chunk=chunk),
        grid=(bsz, M_HEADS),
        in_specs=[blk_t(0), blk_t(M_HEADS), blk_t(2 * M_HEADS),
                  pl.BlockSpec((1, s, M_DK), lambda b, h: (b, 0, h)),
                  pl.BlockSpec((1, nc, 1, 16, chunk), lambda b, h: (b, 0, h, 0, 0)),
                  pl.BlockSpec((1, M_DV, 1), lambda b, h: (h, 0, 0))],
        out_specs=pl.BlockSpec((1, nc, M_DV, chunk), lambda b, h: (b, 0, h, 0)),
        out_shape=jax.ShapeDtypeStruct((bsz, nc, M_HEADS * M_DV, chunk), BF16),
        scratch_shapes=[pltpu.VMEM((nc, M_DV, chunk), F32),
                        pltpu.VMEM((2, M_DV, M_DK), F32),
                        pltpu.VMEM((2, 8, M_DK), F32)],
        compiler_params=_cparams(2),
        name="mlstm_scan",
    )(qvo_t, qvo_t, qvo_t, k_tok, prep, head_g.reshape(M_HEADS, M_DV, 1))


def _mix_ffn_kernel(a_ref, ap_ref, an_ref, x_ref, xp_ref, xn_ref, wo_ref, g1_ref, g_ref, sc_ref, sh_ref,
                    gate_ref, wup_ref, cw_ref, cb_ref, wdn_ref, fg_ref, o_ref, h_scr, a_scr,
                    *, tm, final):
    i = pl.program_id(1)
    last = pl.num_programs(1) - 1
    wo = wo_ref[...]
    g1 = g1_ref[0, 0]

    def mix(a_t):
        return lax.dot_general(a_t, wo, (((0,), (0,)), ((), ())), preferred_element_type=F32)

    a_main = jnp.concatenate([a_ref[0, cc] for cc in range(a_ref.shape[1])], axis=1)
    x1 = x_ref[0] + g1 * mix(a_main)
    x1p = xp_ref[0] + g1 * mix(ap_ref[0, 0][:, LANES - HALO:])
    x1n = xn_ref[0] + g1 * mix(an_ref[0, 0][:, :HALO])

    g = g_ref[...]
    sc = sc_ref[0, 0]
    sh = sh_ref[0, 0]
    hp = jnp.where(i > 0, _norm_mod(x1p, g, sc, sh), 0.0)
    hn = jnp.where(i < last, _norm_mod(x1n, g, sc, sh), 0.0)
    h_scr[0:HALO] = hp.astype(BF16)
    h_scr[HALO:HALO + tm] = _norm_mod(x1, g, sc, sh).astype(BF16)
    h_scr[HALO + tm:] = hn.astype(BF16)
    h = h_scr[...]
    ext = tm + 2 * HALO

    def conv(u, off):
        w0 = cw_ref[0, 0:1, off:off + FFN_CHUNK]
        w1 = cw_ref[0, 1:2, off:off + FFN_CHUNK]
        w2 = cw_ref[0, 2:3, off:off + FFN_CHUNK]
        prev = pltpu.roll(u, 1, axis=0)
        nxt = pltpu.roll(u, ext - 1, axis=0)
        out = w0 * prev + w1 * u + w2 * nxt + cb_ref[0, :, off:off + FFN_CHUNK]
        return out[HALO:HALO + tm]

    for f in range(D_FF // FFN_CHUNK):
        og = f * FFN_CHUNK
        ov = D_FF + f * FFN_CHUNK
        ug = conv(jnp.dot(h, wup_ref[0, :, og:og + FFN_CHUNK], preferred_element_type=F32), og)
        uv = conv(jnp.dot(h, wup_ref[0, :, ov:ov + FFN_CHUNK], preferred_element_type=F32), ov)
        a_scr[:, og:og + FFN_CHUNK] = (ug * jax.nn.sigmoid(ug) * uv).astype(BF16)

    y = jnp.dot(a_scr[...], wdn_ref[0], preferred_element_type=F32)
    out = x1 + gate_ref[0, 0] * y
    if final:
        ms = jnp.mean(out * out, axis=-1, keepdims=True)
        out = out * lax.rsqrt(ms + EPS) * fg_ref[...]
    o_ref[0] = out


def _mix_ffn(a_t, w_o, x, g, mod4, layer, w_up, conv_w, conv_b, w_down, final_g, *, tm, chunk, final):
    bsz, s, d = x.shape
    kdim = w_o.shape[0]
    nc = s // chunk
    cpt = tm // chunk
    hb = tm // HALO
    nh = s // HALO
    return pl.pallas_call(
        functools.partial(_mix_ffn_kernel, tm=tm, final=final),
        grid=(bsz, s // tm),
        in_specs=[pl.BlockSpec((1, cpt, kdim, chunk), lambda b, i: (b, i, 0, 0)),
                  pl.BlockSpec((1, 1, kdim, LANES),
                               lambda b, i: (b, jnp.maximum(i * cpt - 1, 0), 0, chunk // LANES - 1)),
                  pl.BlockSpec((1, 1, kdim, LANES),
                               lambda b, i: (b, jnp.minimum((i + 1) * cpt, nc - 1), 0, 0)),
                  pl.BlockSpec((1, tm, d), lambda b, i: (b, i, 0)),
                  pl.BlockSpec((1, HALO, d), lambda b, i: (b, jnp.maximum(i * hb - 1, 0), 0)),
                  pl.BlockSpec((1, HALO, d), lambda b, i: (b, jnp.minimum((i + 1) * hb, nh - 1), 0)),
                  _resident((kdim, d)),
                  pl.BlockSpec((1, 1, 1, d), lambda b, i: (b, 2, 0, 0)),
                  _resident((1, d)),
                  pl.BlockSpec((1, 1, 1, d), lambda b, i: (b, 4, 0, 0)),
                  pl.BlockSpec((1, 1, 1, d), lambda b, i: (b, 3, 0, 0)),
                  pl.BlockSpec((1, 1, 1, d), lambda b, i: (b, 5, 0, 0)),
                  _resident((d, 2 * D_FF), layer),
                  _resident((3, 2 * D_FF), layer),
                  _resident((1, 2 * D_FF), layer),
                  _resident((D_FF, d), layer),
                  _resident((1, d))],
        out_specs=pl.BlockSpec((1, tm, d), lambda b, i: (b, i, 0)),
        out_shape=jax.ShapeDtypeStruct((bsz, s, d), F32),
        scratch_shapes=[pltpu.VMEM((tm + 2 * HALO, d), BF16),
                        pltpu.VMEM((tm, D_FF), BF16)],
        compiler_params=_cparams(2),
        name="mix_ffn_final" if final else "mix_ffn",
    )(a_t, a_t, a_t, x, x, x, w_o, mod4, g.reshape(1, d), mod4, mod4, mod4, w_up, conv_w,
      conv_b.reshape(DEPTH, 1, -1), w_down, final_g.reshape(1, d))


def _t5_bucket(rel):
    nb = NUM_BUCKETS // 2
    max_exact = nb // 2
    ret = jnp.where(rel > 0, nb, 0)
    n = jnp.abs(rel)
    nf = jnp.maximum(n, 1).astype(jnp.float32)
    large = max_exact + (jnp.log(nf / max_exact) / math.log(MAX_DISTANCE / max_exact)
                         * (nb - max_exact)).astype(jnp.int32)
    large = jnp.minimum(large, nb - 1)
    return ret + jnp.where(n < max_exact, n, large)


N_EDGE = 3


def _bias_kernel(bucket_ref, rb_ref, o_ref):
    hh = pl.program_id(0)
    head = (hh % A_KV) * A_GROUP + hh // A_KV
    bucket = bucket_ref[...]
    c = lax.broadcasted_iota(jnp.int32, bucket.shape, 0)
    q = lax.broadcasted_iota(jnp.int32, bucket.shape, 1)
    band = jnp.abs(c - WINDOW - q) <= WINDOW
    val = jnp.zeros(bucket.shape, F32)
    for kb in range(NUM_BUCKETS):
        val = jnp.where(bucket == kb, rb_ref[kb, head], val)
    val = val * LOG2E
    o_ref[0, 0] = jnp.where(band & (c >= WINDOW), val, -jnp.inf)
    o_ref[1, 0] = jnp.where(band, val, -jnp.inf)
    o_ref[N_EDGE - 1, 0] = jnp.where(band & (c < WINDOW + A_BLOCK), val, -jnp.inf)


def _bias_table(rel_bias):
    c_idx = jnp.arange(A_SPAN)[:, None]
    q_idx = jnp.arange(A_BLOCK)[None, :]
    bucket = _t5_bucket(c_idx - WINDOW - q_idx).astype(jnp.int32)
    return pl.pallas_call(
        _bias_kernel,
        grid=(A_HEADS,),
        in_specs=[pl.BlockSpec((A_SPAN, A_BLOCK), lambda h: (0, 0)),
                  pl.BlockSpec(memory_space=pltpu.SMEM)],
        out_specs=pl.BlockSpec((N_EDGE, 1, A_SPAN, A_BLOCK), lambda h: (0, h, 0, 0)),
        out_shape=jax.ShapeDtypeStruct((N_EDGE, A_HEADS, A_SPAN, A_BLOCK), F32),
        compiler_params=_cparams(1),
        name="bias_table",
    )(bucket, rel_bias)


def _swa_kernel(qt_ref, kp_ref, kc_ref, kn_ref, vp_ref, vc_ref, vn_ref, bias_ref, sink_ref, o_ref):
    j = pl.program_id(1)
    last = pl.num_programs(1) - 1
    n_sub = SWA_SUB
    kblk = [kp_ref[0]] + [kc_ref[0, u * A_BLOCK:(u + 1) * A_BLOCK] for u in range(n_sub)] + [kn_ref[0]]
    vblk = [vp_ref[0, 0]] + [vc_ref[0, u] for u in range(n_sub)] + [vn_ref[0, 0]]
    var = [1] * n_sub
    var[0] = jnp.where(j == 0, 0, 1)
    var[-1] = jnp.where(j == last, N_EDGE - 1, 1)
    lane_kv = lax.broadcasted_iota(jnp.int32, (A_SPAN, A_KV * A_HD), 1) // A_HD
    row_kv = lax.broadcasted_iota(jnp.int32, (A_KV * A_HD, A_SPAN), 0) // A_HD
    grp = lax.broadcasted_iota(jnp.int32, (1, A_GROUP * A_BLOCK), 1) // A_BLOCK
    groups = range(A_GROUP)
    kcat = []
    vstack = []
    qp = []
    for u in range(n_sub):
        kcat.append(jnp.concatenate(kblk[u:u + 3], axis=0))
        vcat = jnp.concatenate(vblk[u:u + 3], axis=1)
        vstack.append(jnp.concatenate([jnp.where(row_kv == kv, vcat, jnp.zeros_like(vcat))
                                       for kv in range(A_KV)], axis=1))
        qp.append(jnp.concatenate([qt_ref[0, u, g * MXU_DIM:(g + 1) * MXU_DIM, :] for g in groups],
                                  axis=1))
    ps = [[] for _ in range(n_sub)]
    rden = [[] for _ in range(n_sub)]
    for kv in range(A_KV):
        sink = jnp.zeros((1, A_GROUP * A_BLOCK), F32)
        for g in groups:
            sink = jnp.where(grp == g, sink_ref[kv * A_GROUP + g], sink)
        sink = sink * LOG2E
        for u in range(n_sub):
            bias = jnp.concatenate([bias_ref[var[u], g * A_KV + kv] for g in groups], axis=1)
            k_kv = jnp.where(lane_kv == kv, kcat[u], jnp.zeros_like(kcat[u]))
            s = jnp.dot(k_kv, qp[u], preferred_element_type=F32) + bias
            m = jnp.maximum(jnp.max(s, axis=0, keepdims=True), sink)
            p = jnp.exp2(s - m)
            rden[u].append(1.0 / (jnp.sum(p, axis=0, keepdims=True) + jnp.exp2(sink - m)))
            ps[u].append(p.astype(BF16))
    for u in range(n_sub):
        out = jnp.dot(vstack[u], jnp.concatenate(ps[u], axis=0), preferred_element_type=F32)
        out = jnp.concatenate([out[kv * A_HD:(kv + 1) * A_HD] * rden[u][kv] for kv in range(A_KV)], axis=0)
        for g in groups:
            o_ref[0, u, g * MXU_DIM:(g + 1) * MXU_DIM, :] = (
                out[:, g * A_BLOCK:(g + 1) * A_BLOCK].astype(BF16))


def _swa(k_tok, qv_t, bias_tbl, sink):
    bsz, s, _ = k_tok.shape
    nb = s // A_BLOCK
    ns = SWA_SUB
    assert nb % ns == 0 and nb >= 2 * ns
    qw = A_HEADS * A_HD
    vrow = qw // MXU_DIM
    prev = lambda j: jnp.maximum(ns * j - 1, 0)
    nxt = lambda j: jnp.minimum(ns * j + ns, nb - 1)
    k_edge = lambda row: pl.BlockSpec((1, A_BLOCK, MXU_DIM), lambda b, j: (b, row(j), 0))
    v_edge = lambda row: pl.BlockSpec((1, 1, MXU_DIM, A_BLOCK), lambda b, j: (b, row(j), vrow, 0))
    return pl.pallas_call(
        _swa_kernel,
        grid=(bsz, nb // ns),
        in_specs=[pl.BlockSpec((1, ns, qw, A_BLOCK), lambda b, j: (b, j, 0, 0)),
                  k_edge(prev), pl.BlockSpec((1, ns * A_BLOCK, MXU_DIM), lambda b, j: (b, j, 0)), k_edge(nxt),
                  v_edge(prev), pl.BlockSpec((1, ns, MXU_DIM, A_BLOCK), lambda b, j: (b, j, vrow, 0)),
                  v_edge(nxt),
                  _resident((N_EDGE, A_HEADS, A_SPAN, A_BLOCK)),
                  pl.BlockSpec(memory_space=pltpu.SMEM)],
        out_specs=pl.BlockSpec((1, ns, qw, A_BLOCK), lambda b, j: (b, j, 0, 0)),
        out_shape=jax.ShapeDtypeStruct((bsz, nb, qw, A_BLOCK), BF16),
        compiler_params=_cparams(2),
        name="swa",
    )(qv_t, k_tok, k_tok, k_tok, qv_t, qv_t, qv_t, bias_tbl, sink)


def _prep_weights(mlstm_w_in, mlstm_b_gate, mlstm_w_out, attn_w_in, attn_w_out, ffn_w_up, ffn_w_down):
    qk = M_HEADS * M_DK
    w_m = mlstm_w_in[0]
    w_q = w_m[:, :qk] * (M_DK ** -0.5)
    perm = lambda g: g.reshape(-1, 2, 2, M_HEADS).swapaxes(1, 2).reshape(-1, N_GATES)
    w_t = jnp.concatenate([w_q, w_m[:, 2 * qk:4 * qk], perm(w_m[:, 4 * qk:])], axis=1).T.astype(BF16)
    w_k = w_m[:, qk:2 * qk].astype(BF16)
    b_gate = jnp.broadcast_to(perm(mlstm_b_gate[0][None]).T, (N_GATES, SCAN_CHUNK))
    qw = A_HEADS * A_HD
    kw = A_KV * A_HD
    w_a = attn_w_in[0]
    wq = w_a[:, :qw].reshape(D_MODEL, A_KV, A_GROUP, A_HD).transpose(0, 2, 1, 3).reshape(D_MODEL, qw)
    w_attn_k = w_a[:, qw:qw + kw].astype(BF16)
    w_attn_t = jnp.concatenate([wq * (A_HD ** -0.5 * LOG2E), w_a[:, qw + kw:]], axis=1).T.astype(BF16)
    wo_attn = attn_w_out[0].reshape(A_KV, A_GROUP, A_HD, D_MODEL).transpose(1, 0, 2, 3)
    wo_attn = wo_attn.reshape(qw, D_MODEL).astype(BF16)
    return dict(w_k=w_k, w_t=w_t, b_gate=b_gate,
                wo_m=mlstm_w_out[0].astype(BF16), w_attn_k=w_attn_k, w_attn_t=w_attn_t, wo_attn=wo_attn,
                w_up=ffn_w_up.astype(BF16), w_down=ffn_w_down.astype(BF16))


def _trunk(x, mod, wts, bias_tbl, norm_g, mlstm_head_g, attn_sink, ffn_conv_w, ffn_conv_b, final_g,
           *, tm, tm_ffn):
    bsz, s, d = x.shape
    mod4 = mod[0].reshape(bsz, 6, 1, d)
    k_tok, qvo_t, gates = _inproj_split(x, norm_g[0, 0], mod4, wts["w_k"], wts["w_t"],
                                        tm=tm, chunk=SCAN_CHUNK, tn=512, n_extra=N_GATES)
    prep = _gateprep(gates, wts["b_gate"], SCAN_CHUNK)
    a_t = _mlstm_scan(k_tok, qvo_t, prep, mlstm_head_g[0], SCAN_CHUNK)
    x = _mix_ffn(a_t, wts["wo_m"], x, norm_g[0, 1], mod4, 0, wts["w_up"], ffn_conv_w, ffn_conv_b,
                 wts["w_down"], final_g, tm=tm_ffn, chunk=SCAN_CHUNK, final=False)
    mod4 = mod[1].reshape(bsz, 6, 1, d)
    k_tok, qv_t = _inproj_split(x, norm_g[1, 0], mod4, wts["w_attn_k"], wts["w_attn_t"],
                                tm=tm, chunk=A_BLOCK, tn=256)
    a_t = _swa(k_tok, qv_t, bias_tbl, attn_sink[0])
    return _mix_ffn(a_t, wts["wo_attn"], x, norm_g[1, 1], mod4, 1, wts["w_up"], ffn_conv_w,
                    ffn_conv_b, wts["w_down"], final_g, tm=tm_ffn, chunk=A_BLOCK, final=True)


def kernel(x_prompt, x_sample, c_prompt, c_sample, adaln_w, adaln_b, norm_g, mlstm_w_in, mlstm_b_gate, mlstm_head_g, mlstm_w_out, attn_w_in, attn_sink, attn_w_out, rel_bias, ffn_w_up, ffn_conv_w, ffn_conv_b, ffn_w_down, final_g):
    nbp = x_prompt.shape[0]
    mod = _adaln(jnp.concatenate([c_prompt, c_sample], axis=0), adaln_w, adaln_b)
    wts = _prep_weights(mlstm_w_in, mlstm_b_gate, mlstm_w_out, attn_w_in, attn_w_out, ffn_w_up, ffn_w_down)
    bias_tbl = _bias_table(rel_bias)
    run = functools.partial(_trunk, wts=wts, bias_tbl=bias_tbl, norm_g=norm_g,
                            mlstm_head_g=mlstm_head_g, attn_sink=attn_sink, ffn_conv_w=ffn_conv_w,
                            ffn_conv_b=ffn_conv_b, final_g=final_g, tm=1024, tm_ffn=1024)
    return run(x_prompt, mod[:, :nbp]), run(x_sample, mod[:, nbp:])
```

```python
import functools
import math

import jax
import jax.numpy as jnp
from jax import lax
from jax.experimental import pallas as pl
from jax.experimental.pallas import tpu as pltpu

F32 = jnp.float32
BF16 = jnp.bfloat16

D_MODEL = 1024
DEPTH = 2
M_HEADS = 4
M_DK = 256
M_DV = 256
A_HEADS = 16
A_KV = 4
A_GROUP = A_HEADS // A_KV
A_HD = 64
WINDOW = 128
A_BLOCK = 128
A_SPAN = A_BLOCK + 2 * WINDOW
NUM_BUCKETS = 32
MAX_DISTANCE = 128
D_FF = 2816
EPS = 1e-6
LOG2E = 1.4426950408889634

LANES = 128
MXU_DIM = 256
VMEM_LIMIT = 56 * 1024 * 1024

SCAN_CHUNK = 256
N_GATES = 4 * M_HEADS
HALO = 8
FFN_CHUNK = MXU_DIM
SWA_SUB = 8


def _cparams(n_axes):
    return pltpu.CompilerParams(dimension_semantics=("parallel",) * n_axes,
                                vmem_limit_bytes=VMEM_LIMIT)


def _resident(shape, layer=None):
    nd = len(shape)
    if layer is None:
        return pl.BlockSpec(shape, lambda *_: (0,) * nd, pipeline_mode=pl.Buffered(1))
    return pl.BlockSpec((1,) + tuple(shape), lambda *_: (layer,) + (0,) * nd,
                        pipeline_mode=pl.Buffered(1))


def _norm_mod(x, g, sc, sh):
    ms = jnp.mean(x * x, axis=-1, keepdims=True)
    return x * lax.rsqrt(ms + EPS) * (g * (1.0 + sc)) + sh


def _adaln_kernel(c_ref, w_ref, b_ref, o_ref):
    c = c_ref[...]
    s = c * jax.nn.sigmoid(c)
    o_ref[0] = jnp.dot(s.astype(BF16), w_ref[0].astype(BF16), preferred_element_type=F32) + b_ref[0]


def _adaln(c_all, adaln_w, adaln_b):
    nb = c_all.shape[0]
    n = adaln_w.shape[-1]
    tn = 1536
    return pl.pallas_call(
        _adaln_kernel,
        grid=(DEPTH, n // tn),
        in_specs=[pl.BlockSpec((nb, D_MODEL), lambda l, j: (0, 0)),
                  pl.BlockSpec((1, D_MODEL, tn), lambda l, j: (l, 0, j)),
                  pl.BlockSpec((1, 1, tn), lambda l, j: (l, 0, j))],
        out_specs=pl.BlockSpec((1, nb, tn), lambda l, j: (l, 0, j)),
        out_shape=jax.ShapeDtypeStruct((DEPTH, nb, n), F32),
        compiler_params=_cparams(2),
        name="adaln",
    )(c_all, adaln_w, adaln_b.reshape(DEPTH, 1, n))


def _inproj_split_kernel(x_ref, g_ref, sc_ref, sh_ref, wk_ref, wt_ref, k_ref, t_ref, *rest,
                         tn, chunk, n_extra):
    h = _norm_mod(x_ref[0], g_ref[...], sc_ref[0, 0], sh_ref[0, 0]).astype(BF16)
    tm = h.shape[0]
    nk = wk_ref.shape[1]
    tk = min(tn, nk)
    for j in range(nk // tk):
        k_ref[0, :, j * tk:(j + 1) * tk] = jnp.dot(
            h, wk_ref[:, j * tk:(j + 1) * tk], preferred_element_type=F32).astype(BF16)
    nt = wt_ref.shape[0] - n_extra
    for j in range(nt // tn):
        hi = (j + 1) * tn + (n_extra if j == nt // tn - 1 else 0)
        res = lax.dot_general(wt_ref[j * tn:hi, :], h, (((1,), (1,)), ((), ())),
                              preferred_element_type=F32)
        for cc in range(tm // chunk):
            cols = slice(cc * chunk, (cc + 1) * chunk)
            t_ref[0, cc, j * tn:(j + 1) * tn, :] = res[:tn, cols].astype(BF16)
            if hi > (j + 1) * tn:
                rest[0][0, cc] = res[tn:, cols]


def _inproj_split(x, g, mod4, w_k, w_t, *, tm, chunk, tn, n_extra=0):
    bsz, s, d = x.shape
    nk = w_k.shape[1]
    nt = w_t.shape[0] - n_extra
    out_specs = [pl.BlockSpec((1, tm, nk), lambda b, i: (b, i, 0)),
                 pl.BlockSpec((1, tm // chunk, nt, chunk), lambda b, i: (b, i, 0, 0))]
    out_shape = [jax.ShapeDtypeStruct((bsz, s, nk), BF16),
                 jax.ShapeDtypeStruct((bsz, s // chunk, nt, chunk), BF16)]
    if n_extra:
        out_specs.append(pl.BlockSpec((1, tm // chunk, n_extra, chunk), lambda b, i: (b, i, 0, 0)))
        out_shape.append(jax.ShapeDtypeStruct((bsz, s // chunk, n_extra, chunk), F32))
    return pl.pallas_call(
        functools.partial(_inproj_split_kernel, tn=tn, chunk=chunk, n_extra=n_extra),
        grid=(bsz, s // tm),
        in_specs=[pl.BlockSpec((1, tm, d), lambda b, i: (b, i, 0)),
                  _resident((1, d)),
                  pl.BlockSpec((1, 1, 1, d), lambda b, i: (b, 1, 0, 0)),
                  pl.BlockSpec((1, 1, 1, d), lambda b, i: (b, 0, 0, 0)),
                  _resident((d, nk)), _resident((nt + n_extra, d))],
        out_specs=out_specs, out_shape=out_shape,
        compiler_params=_cparams(2),
        name="inproj_m" if n_extra else "inproj_a",
    )(x, g.reshape(1, d), mod4, mod4, w_k, w_t)


N_PREP = 6


def _gateprep_kernel(g_ref, b_ref, p_ref, *, nc, chunk):
    s = nc * chunk
    gt = jnp.concatenate([g_ref[0, c] + b_ref[...] for c in range(nc)], axis=1)
    ig = gt[0:8]
    fg = gt[8:16]
    row = lax.broadcasted_iota(jnp.int32, (8, s), 0)
    pos = lax.broadcasted_iota(jnp.int32, (8, s), 1) & (chunk - 1)
    fwd = row < M_HEADS

    def seg_scan(x, op, ident):
        k = 1
        while k < chunk:
            sh_f = jnp.where(pos >= k, pltpu.roll(x, k, axis=1), ident)
            sh_b = jnp.where(pos < chunk - k, pltpu.roll(x, s - k, axis=1), ident)
            x = op(x, jnp.where(fwd, sh_f, sh_b))
            k *= 2
        return x

    lf = jnp.minimum(fg, 0.0) - jnp.log1p(jnp.exp(-jnp.abs(fg)))
    bsum = seg_scan(lf, jnp.add, 0.0)
    beta = ig - bsum
    cmax = seg_scan(beta, jnp.maximum, -jnp.inf)

    fwd1 = fwd[:, :1]
    g_max = [jnp.max(beta[:, c * chunk:(c + 1) * chunk], axis=1, keepdims=True) for c in range(nc)]
    b_last = [jnp.where(fwd1, bsum[:, (c + 1) * chunk - 1:(c + 1) * chunk], bsum[:, c * chunk:c * chunk + 1])
              for c in range(nc)]
    m_f = [None] * nc
    m_b = [None] * nc
    mf = jnp.zeros((8, 1), F32)
    mb = jnp.zeros((8, 1), F32)
    for i in range(nc):
        j = nc - 1 - i
        m_f[i] = mf
        m_b[j] = mb
        mf = b_last[i] + jnp.maximum(mf, g_max[i])
        mb = b_last[j] + jnp.maximum(mb, g_max[j])
    wide = lambda v: jnp.broadcast_to(v, (8, chunk))
    m_in = jnp.concatenate([wide(jnp.where(fwd1, m_f[c], m_b[c])) for c in range(nc)], axis=1)
    g_all = jnp.concatenate([wide(g_max[c]) for c in range(nc)], axis=1)

    mx = jnp.maximum(cmax, m_in)
    m_c = jnp.maximum(m_in, g_all)
    quantities = (beta * LOG2E, mx * LOG2E, jnp.exp(m_in - mx), jnp.exp(-(bsum + mx)), jnp.exp(beta - m_c),
                  jnp.exp(m_in - m_c))
    row16 = lax.broadcasted_iota(jnp.int32, (16, s), 0)
    for h in range(M_HEADS):
        tile = jnp.zeros((16, s), F32)
        for d in range(2):
            for qi, val in enumerate(quantities):
                src = val[d * M_HEADS + h:d * M_HEADS + h + 1]
                tile = jnp.where(row16 == d * 8 + qi, jnp.broadcast_to(src, (16, s)), tile)
        for c in range(nc):
            p_ref[0, c, h] = tile[:, c * chunk:(c + 1) * chunk]


def _gateprep(gates, bias, chunk):
    bsz, nc, ng, _ = gates.shape
    return pl.pallas_call(
        functools.partial(_gateprep_kernel, nc=nc, chunk=chunk),
        grid=(bsz,),
        in_specs=[pl.BlockSpec((1, nc, ng, chunk), lambda b: (b, 0, 0, 0)),
                  _resident((ng, chunk))],
        out_specs=pl.BlockSpec((1, nc, M_HEADS, 16, chunk), lambda b: (b, 0, 0, 0, 0)),
        out_shape=jax.ShapeDtypeStruct((bsz, nc, M_HEADS, 16, chunk), F32),
        compiler_params=_cparams(1),
        name="gateprep",
    )(gates, bias)


def _mlstm_kernel(qt_ref, vt_ref, ot_ref, k_ref, p_ref, hg_ref, at_ref, acc, ct_scr, n_scr,
                  *, nc, chunk):
    L = chunk
    ct_scr[...] = jnp.zeros_like(ct_scr)
    n_scr[...] = jnp.zeros_like(n_scr)
    s_idx = lax.broadcasted_iota(jnp.int32, (L, L), 0)
    t_idx = lax.broadcasted_iota(jnp.int32, (L, L), 1)
    tri = (s_idx <= t_idx, s_idx >= t_idx)
    n_pad = 16

    def body(i, carry, *, finish):
        cs = (i, nc - 1 - i)
        qt = [qt_ref[0, c] for c in cs]
        vt = [vt_ref[0, c] for c in cs]
        k = [k_ref[0, pl.ds(pl.multiple_of(c * L, L), L), :] for c in cs]
        prm = [p_ref[0, cs[d], 0, d * 8:(d + 1) * 8, :] for d in range(2)]
        ct_old = [ct_scr[d] for d in range(2)]
        n_old = [n_scr[d] for d in range(2)]
        r1 = []
        for d in range(2):
            lhs = jnp.concatenate([k[d], ct_old[d].astype(BF16),
                                   jnp.concatenate([n_old[d], n_old[d]], axis=0).astype(BF16)], axis=0)
            r1.append(jnp.dot(lhs, qt[d], preferred_element_type=F32))
        r2 = []
        for d in range(2):
            w = prm[d][4:5]
            vts = (vt[d].astype(F32) * w).astype(BF16)
            lhs2 = jnp.concatenate([vts, jnp.broadcast_to(w, (n_pad, L)).astype(BF16)], axis=0)
            r2.append(jnp.dot(lhs2, k[d], preferred_element_type=F32))
        for d in range(2):
            beta, mx, w_inter, neg_mt, _, decay = (prm[d][j:j + 1] for j in range(N_PREP))
            d_t = jnp.where(tri[d], jnp.exp2(jnp.broadcast_to(beta, (L, L)).T - mx), 0.0)
            p_t = r1[d][0:L] * d_t
            inter = r1[d][L:L + M_DV]
            qn = r1[d][L + M_DV:L + M_DV + 1]
            num = jnp.dot(vt[d], p_t.astype(BF16), preferred_element_type=F32) + w_inter * inter
            den = jnp.sum(p_t, axis=0, keepdims=True) + w_inter * qn
            hout = num * (1.0 / jnp.maximum(jnp.abs(den), neg_mt))
            if finish:
                hs = acc[cs[d]] + hout
                ms = jnp.mean(hs * hs, axis=0, keepdims=True)
                y = hs * lax.rsqrt(ms + EPS) * hg_ref[0]
                gate = jax.nn.sigmoid(ot_ref[0, cs[d]].astype(F32))
                at_ref[0, cs[d]] = (y * gate).astype(BF16)
            else:
                acc[cs[d]] = hout
            ct_scr[d] = decay * ct_old[d] + r2[d][0:M_DV]
            n_scr[d] = decay * n_old[d] + r2[d][M_DV:M_DV + 8]
        return carry

    half = nc // 2
    lax.fori_loop(0, half, functools.partial(body, finish=False), 0, unroll=2)
    lax.fori_loop(half, nc, functools.partial(body, finish=True), 0, unroll=2)


def _mlstm_scan(k_tok, qvo_t, prep, head_g, chunk):
    bsz, s, _ = k_tok.shape
    nc = s // chunk
    assert nc % 4 == 0
    blk_t = lambda off: pl.BlockSpec((1, nc, M_DK, chunk), lambda b, h: (b, 0, off + h, 0))
    return pl.pallas_call(
        functools.partial(_mlstm_kernel, nc=nc, chunk=chunk),
        grid=(bsz, M_HEADS),
        in_specs=[blk_t(0), blk_t(M_HEADS), blk_t(2 * M_HEADS),
                  pl.BlockSpec((1, s, M_DK), lambda b, h: (b, 0, h)),
                  pl.BlockSpec((1, nc, 1, 16, chunk), lambda b, h: (b, 0, h, 0, 0)),
                  pl.BlockSpec((1, M_DV, 1), lambda b, h: (h, 0, 0))],
        out_specs=pl.BlockSpec((1, nc, M_DV, chunk), lambda b, h: (b, 0, h, 0)),
        out_shape=jax.ShapeDtypeStruct((bsz, nc, M_HEADS * M_DV, chunk), BF16),
        scratch_shapes=[pltpu.VMEM((nc, M_DV, chunk), F32),
                        pltpu.VMEM((2, M_DV, M_DK), F32),
                        pltpu.VMEM((2, 8, M_DK), F32)],
        compiler_params=_cparams(2),
        name="mlstm_scan",
    )(qvo_t, qvo_t, qvo_t, k_tok, prep, head_g.reshape(M_HEADS, M_DV, 1))


def _mix_ffn_kernel(a_ref, ap_ref, an_ref, x_ref, xp_ref, xn_ref, wo_ref, g1_ref, g_ref, sc_ref, sh_ref,
                    gate_ref, wup_ref, cw_ref, cb_ref, wdn_ref, fg_ref, o_ref, h_scr, a_scr,
                    *, tm, final):
    i = pl.program_id(1)
    last = pl.num_programs(1) - 1
    wo = wo_ref[...]
    g1 = g1_ref[0, 0]

    def mix(a_t):
        return lax.dot_general(a_t, wo, (((0,), (0,)), ((), ())), preferred_element_type=F32)

    a_main = jnp.concatenate([a_ref[0, cc] for cc in range(a_ref.shape[1])], axis=1)
    x1 = x_ref[0] + g1 * mix(a_main)
    x1p = xp_ref[0] + g1 * mix(ap_ref[0, 0][:, LANES - HALO:])
    x1n = xn_ref[0] + g1 * mix(an_ref[0, 0][:, :HALO])

    g = g_ref[...]
    sc = sc_ref[0, 0]
    sh = sh_ref[0, 0]
    hp = jnp.where(i > 0, _norm_mod(x1p, g, sc, sh), 0.0)
    hn = jnp.where(i < last, _norm_mod(x1n, g, sc, sh), 0.0)
    h_scr[0:2 * HALO] = jnp.concatenate([hn, hp], axis=0).astype(BF16)
    h_scr[2 * HALO:] = _norm_mod(x1, g, sc, sh).astype(BF16)
    h = h_scr[...]
    ext = tm + 2 * HALO

    def conv(u, off):
        w0 = cw_ref[0, 0:1, off:off + FFN_CHUNK]
        w1 = cw_ref[0, 1:2, off:off + FFN_CHUNK]
        w2 = cw_ref[0, 2:3, off:off + FFN_CHUNK]
        prev = pltpu.roll(u, 1, axis=0)
        nxt = pltpu.roll(u, ext - 1, axis=0)
        out = w0 * prev + w1 * u + w2 * nxt + cb_ref[0, :, off:off + FFN_CHUNK]
        return out[2 * HALO:]

    for f in range(D_FF // FFN_CHUNK):
        og = f * FFN_CHUNK
        ov = D_FF + f * FFN_CHUNK
        ug = conv(jnp.dot(h, wup_ref[0, :, og:og + FFN_CHUNK], preferred_element_type=F32), og)
        uv = conv(jnp.dot(h, wup_ref[0, :, ov:ov + FFN_CHUNK], preferred_element_type=F32), ov)
        a_scr[:, og:og + FFN_CHUNK] = (ug * jax.nn.sigmoid(ug) * uv).astype(BF16)

    y = jnp.dot(a_scr[...], wdn_ref[0], preferred_element_type=F32)
    out = x1 + gate_ref[0, 0] * y
    if final:
        ms = jnp.mean(out * out, axis=-1, keepdims=True)
        out = out * lax.rsqrt(ms + EPS) * fg_ref[...]
    o_ref[0] = out


def _mix_ffn(a_t, w_o, x, g, mod4, layer, w_up, conv_w, conv_b, w_down, final_g, *, tm, chunk, final):
    bsz, s, d = x.shape
    kdim = w_o.shape[0]
    nc = s // chunk
    cpt = tm // chunk
    hb = tm // HALO
    nh = s // HALO
    return pl.pallas_call(
        functools.partial(_mix_ffn_kernel, tm=tm, final=final),
        grid=(bsz, s // tm),
        in_specs=[pl.BlockSpec((1, cpt, kdim, chunk), lambda b, i: (b, i, 0, 0)),
                  pl.BlockSpec((1, 1, kdim, LANES),
                               lambda b, i: (b, jnp.maximum(i * cpt - 1, 0), 0, chunk // LANES - 1)),
                  pl.BlockSpec((1, 1, kdim, LANES),
                               lambda b, i: (b, jnp.minimum((i + 1) * cpt, nc - 1), 0, 0)),
                  pl.BlockSpec((1, tm, d), lambda b, i: (b, i, 0)),
                  pl.BlockSpec((1, HALO, d), lambda b, i: (b, jnp.maximum(i * hb - 1, 0), 0)),
                  pl.BlockSpec((1, HALO, d), lambda b, i: (b, jnp.minimum((i + 1) * hb, nh - 1), 0)),
                  _resident((kdim, d)),
                  pl.BlockSpec((1, 1, 1, d), lambda b, i: (b, 2, 0, 0)),
                  _resident((1, d)),
                  pl.BlockSpec((1, 1, 1, d), lambda b, i: (b, 4, 0, 0)),
                  pl.BlockSpec((1, 1, 1, d), lambda b, i: (b, 3, 0, 0)),
                  pl.BlockSpec((1, 1, 1, d), lambda b, i: (b, 5, 0, 0)),
                  _resident((d, 2 * D_FF), layer),
                  _resident((3, 2 * D_FF), layer),
                  _resident((1, 2 * D_FF), layer),
                  _resident((D_FF, d), layer),
                  _resident((1, d))],
        out_specs=pl.BlockSpec((1, tm, d), lambda b, i: (b, i, 0)),
        out_shape=jax.ShapeDtypeStruct((bsz, s, d), F32),
        scratch_shapes=[pltpu.VMEM((tm + 2 * HALO, d), BF16),
                        pltpu.VMEM((tm, D_FF), BF16)],
        compiler_params=_cparams(2),
        name="mix_ffn_final" if final else "mix_ffn",
    )(a_t, a_t, a_t, x, x, x, w_o, mod4, g.reshape(1, d), mod4, mod4, mod4, w_up, conv_w,
      conv_b.reshape(DEPTH, 1, -1), w_down, final_g.reshape(1, d))


def _t5_bucket(rel):
    nb = NUM_BUCKETS // 2
    max_exact = nb // 2
    ret = jnp.where(rel > 0, nb, 0)
    n = jnp.abs(rel)
    nf = jnp.maximum(n, 1).astype(jnp.float32)
    large = max_exact + (jnp.log(nf / max_exact) / math.log(MAX_DISTANCE / max_exact)
                         * (nb - max_exact)).astype(jnp.int32)
    large = jnp.minimum(large, nb - 1)
    return ret + jnp.where(n < max_exact, n, large)


N_EDGE = 3


def _bias_kernel(bucket_ref, rb_ref, o_ref):
    hh = pl.program_id(0)
    head = (hh % A_KV) * A_GROUP + hh // A_KV
    bucket = bucket_ref[...]
    c = lax.broadcasted_iota(jnp.int32, bucket.shape, 0)
    q = lax.broadcasted_iota(jnp.int32, bucket.shape, 1)
    band = jnp.abs(c - WINDOW - q) <= WINDOW
    val = jnp.zeros(bucket.shape, F32)
    for kb in range(NUM_BUCKETS):
        val = jnp.where(bucket == kb, rb_ref[kb, head], val)
    val = val * LOG2E
    o_ref[0, 0] = jnp.where(band & (c >= WINDOW), val, -jnp.inf)
    o_ref[1, 0] = jnp.where(band, val, -jnp.inf)
    o_ref[N_EDGE - 1, 0] = jnp.where(band & (c < WINDOW + A_BLOCK), val, -jnp.inf)


def _bias_table(rel_bias):
    c_idx = jnp.arange(A_SPAN)[:, None]
    q_idx = jnp.arange(A_BLOCK)[None, :]
    bucket = _t5_bucket(c_idx - WINDOW - q_idx).astype(jnp.int32)
    return pl.pallas_call(
        _bias_kernel,
        grid=(A_HEADS,),
        in_specs=[pl.BlockSpec((A_SPAN, A_BLOCK), lambda h: (0, 0)),
                  pl.BlockSpec(memory_space=pltpu.SMEM)],
        out_specs=pl.BlockSpec((N_EDGE, 1, A_SPAN, A_BLOCK), lambda h: (0, h, 0, 0)),
        out_shape=jax.ShapeDtypeStruct((N_EDGE, A_HEADS, A_SPAN, A_BLOCK), F32),
        compiler_params=_cparams(1),
        name="bias_table",
    )(bucket, rel_bias)


def _swa_kernel(qt_ref, kp_ref, kc_ref, kn_ref, vp_ref, vc_ref, vn_ref, bias_ref, sink_ref, o_ref):
    j = pl.program_id(1)
    last = pl.num_programs(1) - 1
    n_sub = SWA_SUB
    kblk = [kp_ref[0]] + [kc_ref[0, u * A_BLOCK:(u + 1) * A_BLOCK] for u in range(n_sub)] + [kn_ref[0]]
    vblk = [vp_ref[0, 0]] + [vc_ref[0, u] for u in range(n_sub)] + [vn_ref[0, 0]]
    var = [1] * n_sub
    var[0] = jnp.where(j == 0, 0, 1)
    var[-1] = jnp.where(j == last, N_EDGE - 1, 1)
    lane_kv = lax.broadcasted_iota(jnp.int32, (A_SPAN, A_KV * A_HD), 1) // A_HD
    row_kv = lax.broadcasted_iota(jnp.int32, (A_KV * A_HD, A_SPAN), 0) // A_HD
    grp = lax.broadcasted_iota(jnp.int32, (1, A_GROUP * A_BLOCK), 1) // A_BLOCK
    groups = range(A_GROUP)
    kcat = []
    vstack = []
    qp = []
    for u in range(n_sub):
        kcat.append(jnp.concatenate(kblk[u:u + 3], axis=0))
        vcat = jnp.concatenate(vblk[u:u + 3], axis=1)
        vstack.append(jnp.concatenate([jnp.where(row_kv == kv, vcat, jnp.zeros_like(vcat))
                                       for kv in range(A_KV)], axis=1))
        qp.append(jnp.concatenate([qt_ref[0, u, g * MXU_DIM:(g + 1) * MXU_DIM, :] for g in groups],
                                  axis=1))
    ps = [[] for _ in range(n_sub)]
    rden = [[] for _ in range(n_sub)]
    for kv in range(A_KV):
        sink = jnp.zeros((1, A_GROUP * A_BLOCK), F32)
        for g in groups:
            sink = jnp.where(grp == g, sink_ref[kv * A_GROUP + g], sink)
        sink = sink * LOG2E
        for u in range(n_sub):
            bias = jnp.concatenate([bias_ref[var[u], g * A_KV + kv] for g in groups], axis=1)
            k_kv = jnp.where(lane_kv == kv, kcat[u], jnp.zeros_like(kcat[u]))
            s = jnp.dot(k_kv, qp[u], preferred_element_type=F32) + bias
            m = jnp.maximum(jnp.max(s, axis=0, keepdims=True), sink)
            p = jnp.exp2(s - m)
            rden[u].append(1.0 / (jnp.sum(p, axis=0, keepdims=True) + jnp.exp2(sink - m)))
            ps[u].append(p.astype(BF16))
    for u in range(n_sub):
        out = jnp.dot(vstack[u], jnp.concatenate(ps[u], axis=0), preferred_element_type=F32)
        out = jnp.concatenate([out[kv * A_HD:(kv + 1) * A_HD] * rden[u][kv] for kv in range(A_KV)], axis=0)
        for g in groups:
            o_ref[0, u, g * MXU_DIM:(g + 1) * MXU_DIM, :] = (
                out[:, g * A_BLOCK:(g + 1) * A_BLOCK].astype(BF16))


def _swa(k_tok, qv_t, bias_tbl, sink):
    bsz, s, _ = k_tok.shape
    nb = s // A_BLOCK
    ns = SWA_SUB
    assert nb % ns == 0 and nb >= 2 * ns
    qw = A_HEADS * A_HD
    vrow = qw // MXU_DIM
    prev = lambda j: jnp.maximum(ns * j - 1, 0)
    nxt = lambda j: jnp.minimum(ns * j + ns, nb - 1)
    k_edge = lambda row: pl.BlockSpec((1, A_BLOCK, MXU_DIM), lambda b, j: (b, row(j), 0))
    v_edge = lambda row: pl.BlockSpec((1, 1, MXU_DIM, A_BLOCK), lambda b, j: (b, row(j), vrow, 0))
    return pl.pallas_call(
        _swa_kernel,
        grid=(bsz, nb // ns),
        in_specs=[pl.BlockSpec((1, ns, qw, A_BLOCK), lambda b, j: (b, j, 0, 0)),
                  k_edge(prev), pl.BlockSpec((1, ns * A_BLOCK, MXU_DIM), lambda b, j: (b, j, 0)), k_edge(nxt),
                  v_edge(prev), pl.BlockSpec((1, ns, MXU_DIM, A_BLOCK), lambda b, j: (b, j, vrow, 0)),
                  v_edge(nxt),
                  _resident((N_EDGE, A_HEADS, A_SPAN, A_BLOCK)),
                  pl.BlockSpec(memory_space=pltpu.SMEM)],
        out_specs=pl.BlockSpec((1, ns, qw, A_BLOCK), lambda b, j: (b, j, 0, 0)),
        out_shape=jax.ShapeDtypeStruct((bsz, nb, qw, A_BLOCK), BF16),
        compiler_params=_cparams(2),
        name="swa",
    )(qv_t, k_tok, k_tok, k_tok, qv_t, qv_t, qv_t, bias_tbl, sink)


def _prep_weights(mlstm_w_in, mlstm_b_gate, mlstm_w_out, attn_w_in, attn_w_out, ffn_w_up, ffn_w_down):
    qk = M_HEADS * M_DK
    w_m = mlstm_w_in[0]
    w_q = w_m[:, :qk] * (M_DK ** -0.5)
    perm = lambda g: g.reshape(-1, 2, 2, M_HEADS).swapaxes(1, 2).reshape(-1, N_GATES)
    w_t = jnp.concatenate([w_q, w_m[:, 2 * qk:4 * qk], perm(w_m[:, 4 * qk:])], axis=1).T.astype(BF16)
    w_k = w_m[:, qk:2 * qk].astype(BF16)
    b_gate = jnp.broadcast_to(perm(mlstm_b_gate[0][None]).T, (N_GATES, SCAN_CHUNK))
    qw = A_HEADS * A_HD
    kw = A_KV * A_HD
    w_a = attn_w_in[0]
    wq = w_a[:, :qw].reshape(D_MODEL, A_KV, A_GROUP, A_HD).transpose(0, 2, 1, 3).reshape(D_MODEL, qw)
    w_attn_k = w_a[:, qw:qw + kw].astype(BF16)
    w_attn_t = jnp.concatenate([wq * (A_HD ** -0.5 * LOG2E), w_a[:, qw + kw:]], axis=1).T.astype(BF16)
    wo_attn = attn_w_out[0].reshape(A_KV, A_GROUP, A_HD, D_MODEL).transpose(1, 0, 2, 3)
    wo_attn = wo_attn.reshape(qw, D_MODEL).astype(BF16)
    return dict(w_k=w_k, w_t=w_t, b_gate=b_gate,
                wo_m=mlstm_w_out[0].astype(BF16), w_attn_k=w_attn_k, w_attn_t=w_attn_t, wo_attn=wo_attn,
                w_up=ffn_w_up.astype(BF16), w_down=ffn_w_down.astype(BF16))


def _trunk(x, mod, wts, bias_tbl, norm_g, mlstm_head_g, attn_sink, ffn_conv_w, ffn_conv_b, final_g,
           *, tm, tm_ffn):
    bsz, s, d = x.shape
    mod4 = mod[0].reshape(bsz, 6, 1, d)
    k_tok, qvo_t, gates = _inproj_split(x, norm_g[0, 0], mod4, wts["w_k"], wts["w_t"],
                                        tm=tm, chunk=SCAN_CHUNK, tn=512, n_extra=N_GATES)
    prep = _gateprep(gates, wts["b_gate"], SCAN_CHUNK)
    a_t = _mlstm_scan(k_tok, qvo_t, prep, mlstm_head_g[0], SCAN_CHUNK)
    x = _mix_ffn(a_t, wts["wo_m"], x, norm_g[0, 1], mod4, 0, wts["w_up"], ffn_conv_w, ffn_conv_b,
                 wts["w_down"], final_g, tm=tm_ffn, chunk=SCAN_CHUNK, final=False)
    mod4 = mod[1].reshape(bsz, 6, 1, d)
    k_tok, qv_t = _inproj_split(x, norm_g[1, 0], mod4, wts["w_attn_k"], wts["w_attn_t"],
                                tm=tm, chunk=A_BLOCK, tn=256)
    a_t = _swa(k_tok, qv_t, bias_tbl, attn_sink[0])
    return _mix_ffn(a_t, wts["wo_attn"], x, norm_g[1, 1], mod4, 1, wts["w_up"], ffn_conv_w,
                    ffn_conv_b, wts["w_down"], final_g, tm=tm_ffn, chunk=A_BLOCK, final=True)


def kernel(x_prompt, x_sample, c_prompt, c_sample, adaln_w, adaln_b, norm_g, mlstm_w_in, mlstm_b_gate, mlstm_head_g, mlstm_w_out, attn_w_in, attn_sink, attn_w_out, rel_bias, ffn_w_up, ffn_conv_w, ffn_conv_b, ffn_w_down, final_g):
    nbp = x_prompt.shape[0]
    mod = _adaln(jnp.concatenate([c_prompt, c_sample], axis=0), adaln_w, adaln_b)
    wts = _prep_weights(mlstm_w_in, mlstm_b_gate, mlstm_w_out, attn_w_in, attn_w_out, ffn_w_up, ffn_w_down)
    bias_tbl = _bias_table(rel_bias)
    run = functools.partial(_trunk, wts=wts, bias_tbl=bias_tbl, norm_g=norm_g,
                            mlstm_head_g=mlstm_head_g, attn_sink=attn_sink, ffn_conv_w=ffn_conv_w,
                            ffn_conv_b=ffn_conv_b, final_g=final_g, tm=1024, tm_ffn=1024)
    return run(x_prompt, mod[:, :nbp]), run(x_sample, mod[:, nbp:])
```

```python
import functools
import math

import jax
import jax.numpy as jnp
from jax import lax
from jax.experimental import pallas as pl
from jax.experimental.pallas import tpu as pltpu

F32 = jnp.float32
BF16 = jnp.bfloat16

D_MODEL = 1024
DEPTH = 2
M_HEADS = 4
M_DK = 256
M_DV = 256
A_HEADS = 16
A_KV = 4
A_GROUP = A_HEADS // A_KV
A_HD = 64
WINDOW = 128
A_BLOCK = 128
A_SPAN = A_BLOCK + 2 * WINDOW
NUM_BUCKETS = 32
MAX_DISTANCE = 128
D_FF = 2816
EPS = 1e-6
LOG2E = 1.4426950408889634

LANES = 128
MXU_DIM = 256
VMEM_LIMIT = 56 * 1024 * 1024

SCAN_CHUNK = 256
N_GATES = 4 * M_HEADS
HALO = 8
FFN_CHUNK = MXU_DIM
SWA_SUB = 8


def _cparams(n_axes):
    return pltpu.CompilerParams(dimension_semantics=("parallel",) * n_axes,
                                vmem_limit_bytes=VMEM_LIMIT)


def _resident(shape, layer=None):
    nd = len(shape)
    if layer is None:
        return pl.BlockSpec(shape, lambda *_: (0,) * nd, pipeline_mode=pl.Buffered(1))
    return pl.BlockSpec((1,) + tuple(shape), lambda *_: (layer,) + (0,) * nd,
                        pipeline_mode=pl.Buffered(1))


def _norm_mod(x, g, sc, sh):
    ms = jnp.mean(x * x, axis=-1, keepdims=True)
    return x * lax.rsqrt(ms + EPS) * (g * (1.0 + sc)) + sh


def _adaln_kernel(c_ref, w_ref, b_ref, o_ref):
    c = c_ref[...]
    s = c * jax.nn.sigmoid(c)
    o_ref[0] = jnp.dot(s.astype(BF16), w_ref[0].astype(BF16), preferred_element_type=F32) + b_ref[0]


def _adaln(c_all, adaln_w, adaln_b):
    nb = c_all.shape[0]
    n = adaln_w.shape[-1]
    tn = 1536
    return pl.pallas_call(
        _adaln_kernel,
        grid=(DEPTH, n // tn),
        in_specs=[pl.BlockSpec((nb, D_MODEL), lambda l, j: (0, 0)),
                  pl.BlockSpec((1, D_MODEL, tn), lambda l, j: (l, 0, j)),
                  pl.BlockSpec((1, 1, tn), lambda l, j: (l, 0, j))],
        out_specs=pl.BlockSpec((1, nb, tn), lambda l, j: (l, 0, j)),
        out_shape=jax.ShapeDtypeStruct((DEPTH, nb, n), F32),
        compiler_params=_cparams(2),
        name="adaln",
    )(c_all, adaln_w, adaln_b.reshape(DEPTH, 1, n))


def _inproj_split_kernel(x_ref, g_ref, sc_ref, sh_ref, wk_ref, wt_ref, k_ref, t_ref, *rest,
                         tn, chunk, n_extra):
    h = _norm_mod(x_ref[0], g_ref[...], sc_ref[0, 0], sh_ref[0, 0]).astype(BF16)
    tm = h.shape[0]
    nk = wk_ref.shape[1]
    tk = min(tn, nk)
    hpt = tk // MXU_DIM
    for j in range(nk // tk):
        res = jnp.dot(h, wk_ref[:, j * tk:(j + 1) * tk], preferred_element_type=F32).astype(BF16)
        for hh in range(hpt):
            k_ref[0, j * hpt + hh] = res[:, hh * MXU_DIM:(hh + 1) * MXU_DIM]
    nt = wt_ref.shape[0] - n_extra
    for j in range(nt // tn):
        hi = (j + 1) * tn + (n_extra if j == nt // tn - 1 else 0)
        res = lax.dot_general(wt_ref[j * tn:hi, :], h, (((1,), (1,)), ((), ())),
                              preferred_element_type=F32)
        for cc in range(tm // chunk):
            cols = slice(cc * chunk, (cc + 1) * chunk)
            t_ref[0, cc, j * tn:(j + 1) * tn, :] = res[:tn, cols].astype(BF16)
            if hi > (j + 1) * tn:
                rest[0][0, cc] = res[tn:, cols]


def _inproj_split(x, g, mod4, w_k, w_t, *, tm, chunk, tn, n_extra=0):
    bsz, s, d = x.shape
    nk = w_k.shape[1]
    nt = w_t.shape[0] - n_extra
    out_specs = [pl.BlockSpec((1, nk // MXU_DIM, tm, MXU_DIM), lambda b, i: (b, 0, i, 0)),
                 pl.BlockSpec((1, tm // chunk, nt, chunk), lambda b, i: (b, i, 0, 0))]
    out_shape = [jax.ShapeDtypeStruct((bsz, nk // MXU_DIM, s, MXU_DIM), BF16),
                 jax.ShapeDtypeStruct((bsz, s // chunk, nt, chunk), BF16)]
    if n_extra:
        out_specs.append(pl.BlockSpec((1, tm // chunk, n_extra, chunk), lambda b, i: (b, i, 0, 0)))
        out_shape.append(jax.ShapeDtypeStruct((bsz, s // chunk, n_extra, chunk), F32))
    return pl.pallas_call(
        functools.partial(_inproj_split_kernel, tn=tn, chunk=chunk, n_extra=n_extra),
        grid=(bsz, s // tm),
        in_specs=[pl.BlockSpec((1, tm, d), lambda b, i: (b, i, 0)),
                  _resident((1, d)),
                  pl.BlockSpec((1, 1, 1, d), lambda b, i: (b, 1, 0, 0)),
                  pl.BlockSpec((1, 1, 1, d), lambda b, i: (b, 0, 0, 0)),
                  _resident((d, nk)), _resident((nt + n_extra, d))],
        out_specs=out_specs, out_shape=out_shape,
        compiler_params=_cparams(2),
        name="inproj_m" if n_extra else "inproj_a",
    )(x, g.reshape(1, d), mod4, mod4, w_k, w_t)


N_PREP = 6


def _gateprep_kernel(g_ref, b_ref, p_ref, *, nc, chunk):
    s = nc * chunk
    gt = jnp.concatenate([g_ref[0, c] + b_ref[...] for c in range(nc)], axis=1)
    ig = gt[0:8]
    fg = gt[8:16]
    row = lax.broadcasted_iota(jnp.int32, (8, s), 0)
    pos = lax.broadcasted_iota(jnp.int32, (8, s), 1) & (chunk - 1)
    fwd = row < M_HEADS

    def seg_scan(x, op, ident):
        k = 1
        while k < chunk:
            sh_f = jnp.where(pos >= k, pltpu.roll(x, k, axis=1), ident)
            sh_b = jnp.where(pos < chunk - k, pltpu.roll(x, s - k, axis=1), ident)
            x = op(x, jnp.where(fwd, sh_f, sh_b))
            k *= 2
        return x

    lf = jnp.minimum(fg, 0.0) - jnp.log1p(jnp.exp(-jnp.abs(fg)))
    bsum = seg_scan(lf, jnp.add, 0.0)
    beta = ig - bsum
    cmax = seg_scan(beta, jnp.maximum, -jnp.inf)

    fwd1 = fwd[:, :1]
    g_max = [jnp.max(beta[:, c * chunk:(c + 1) * chunk], axis=1, keepdims=True) for c in range(nc)]
    b_last = [jnp.where(fwd1, bsum[:, (c + 1) * chunk - 1:(c + 1) * chunk], bsum[:, c * chunk:c * chunk + 1])
              for c in range(nc)]
    m_f = [None] * nc
    m_b = [None] * nc
    mf = jnp.zeros((8, 1), F32)
    mb = jnp.zeros((8, 1), F32)
    for i in range(nc):
        j = nc - 1 - i
        m_f[i] = mf
        m_b[j] = mb
        mf = b_last[i] + jnp.maximum(mf, g_max[i])
        mb = b_last[j] + jnp.maximum(mb, g_max[j])
    wide = lambda v: jnp.broadcast_to(v, (8, chunk))
    m_in = jnp.concatenate([wide(jnp.where(fwd1, m_f[c], m_b[c])) for c in range(nc)], axis=1)
    g_all = jnp.concatenate([wide(g_max[c]) for c in range(nc)], axis=1)

    mx = jnp.maximum(cmax, m_in)
    m_c = jnp.maximum(m_in, g_all)
    quantities = (beta * LOG2E, mx * LOG2E, jnp.exp(m_in - mx), jnp.exp(-(bsum + mx)), jnp.exp(beta - m_c),
                  jnp.exp(m_in - m_c))
    row16 = lax.broadcasted_iota(jnp.int32, (16, s), 0)
    for h in range(M_HEADS):
        tile = jnp.zeros((16, s), F32)
        for d in range(2):
            for qi, val in enumerate(quantities):
                src = val[d * M_HEADS + h:d * M_HEADS + h + 1]
                tile = jnp.where(row16 == d * 8 + qi, jnp.broadcast_to(src, (16, s)), tile)
        for c in range(nc):
            p_ref[0, c, h] = tile[:, c * chunk:(c + 1) * chunk]


def _gateprep(gates, bias, chunk):
    bsz, nc, ng, _ = gates.shape
    return pl.pallas_call(
        functools.partial(_gateprep_kernel, nc=nc, chunk=chunk),
        grid=(bsz,),
        in_specs=[pl.BlockSpec((1, nc, ng, chunk), lambda b: (b, 0, 0, 0)),
                  _resident((ng, chunk))],
        out_specs=pl.BlockSpec((1, nc, M_HEADS, 16, chunk), lambda b: (b, 0, 0, 0, 0)),
        out_shape=jax.ShapeDtypeStruct((bsz, nc, M_HEADS, 16, chunk), F32),
        compiler_params=_cparams(1),
        name="gateprep",
    )(gates, bias)


def _mlstm_kernel(qt_ref, vt_ref, ot_ref, k_ref, p_ref, hg_ref, at_ref, acc, ct_scr, n_scr,
                  *, nc, chunk):
    L = chunk
    ct_scr[...] = jnp.zeros_like(ct_scr)
    n_scr[...] = jnp.zeros_like(n_scr)
    s_idx = lax.broadcasted_iota(jnp.int32, (L, L), 0)
    t_idx = lax.broadcasted_iota(jnp.int32, (L, L), 1)
    tri = (s_idx <= t_idx, s_idx >= t_idx)
    n_pad = 16

    def body(i, carry, *, finish):
        cs = (i, nc - 1 - i)
        qt = [qt_ref[0, c] for c in cs]
        vt = [vt_ref[0, c] for c in cs]
        k = [k_ref[0, 0, pl.ds(pl.multiple_of(c * L, L), L), :] for c in cs]
        prm = [p_ref[0, cs[d], 0, d * 8:(d + 1) * 8, :] for d in range(2)]
        ct_old = [ct_scr[d] for d in range(2)]
        n_old = [n_scr[d] for d in range(2)]
        r1 = []
        for d in range(2):
            lhs = jnp.concatenate([k[d], ct_old[d].astype(BF16),
                                   jnp.concatenate([n_old[d], n_old[d]], axis=0).astype(BF16)], axis=0)
            r1.append(jnp.dot(lhs, qt[d], preferred_element_type=F32))
        r2 = []
        for d in range(2):
            w = prm[d][4:5]
            vts = (vt[d].astype(F32) * w).astype(BF16)
            lhs2 = jnp.concatenate([vts, jnp.broadcast_to(w, (n_pad, L)).astype(BF16)], axis=0)
            r2.append(jnp.dot(lhs2, k[d], preferred_element_type=F32))
        for d in range(2):
            beta, mx, w_inter, neg_mt, _, decay = (prm[d][j:j + 1] for j in range(N_PREP))
            d_t = jnp.where(tri[d], jnp.exp2(jnp.broadcast_to(beta, (L, L)).T - mx), 0.0)
            p_t = r1[d][0:L] * d_t
            inter = r1[d][L:L + M_DV]
            qn = r1[d][L + M_DV:L + M_DV + 1]
            num = jnp.dot(vt[d], p_t.astype(BF16), preferred_element_type=F32) + w_inter * inter
            den = jnp.sum(p_t, axis=0, keepdims=True) + w_inter * qn
            hout = num * (1.0 / jnp.maximum(jnp.abs(den), neg_mt))
            if finish:
                hs = acc[cs[d]] + hout
                ms = jnp.mean(hs * hs, axis=0, keepdims=True)
                y = hs * lax.rsqrt(ms + EPS) * hg_ref[0]
                gate = jax.nn.sigmoid(ot_ref[0, cs[d]].astype(F32))
                at_ref[0, cs[d]] = (y * gate).astype(BF16)
            else:
                acc[cs[d]] = hout
            ct_scr[d] = decay * ct_old[d] + r2[d][0:M_DV]
            n_scr[d] = decay * n_old[d] + r2[d][M_DV:M_DV + 8]
        return carry

    half = nc // 2
    lax.fori_loop(0, half, functools.partial(body, finish=False), 0, unroll=2)
    lax.fori_loop(half, nc, functools.partial(body, finish=True), 0, unroll=2)


def _mlstm_scan(k_tok, qvo_t, prep, head_g, chunk):
    bsz, _, s, _ = k_tok.shape
    nc = s // chunk
    assert nc % 4 == 0
    blk_t = lambda off: pl.BlockSpec((1, nc, M_DK, chunk), lambda b, h: (b, 0, off + h, 0))
    return pl.pallas_call(
        functools.partial(_mlstm_kernel, nc=nc, chunk=chunk),
        grid=(bsz, M_HEADS),
        in_specs=[blk_t(0), blk_t(M_HEADS), blk_t(2 * M_HEADS),
                  pl.BlockSpec((1, 1, s, M_DK), lambda b, h: (b, h, 0, 0)),
                  pl.BlockSpec((1, nc, 1, 16, chunk), lambda b, h: (b, 0, h, 0, 0)),
                  pl.BlockSpec((1, M_DV, 1), lambda b, h: (h, 0, 0))],
        out_specs=pl.BlockSpec((1, nc, M_DV, chunk), lambda b, h: (b, 0, h, 0)),
        out_shape=jax.ShapeDtypeStruct((bsz, nc, M_HEADS * M_DV, chunk), BF16),
        scratch_shapes=[pltpu.VMEM((nc, M_DV, chunk), F32),
                        pltpu.VMEM((2, M_DV, M_DK), F32),
                        pltpu.VMEM((2, 8, M_DK), F32)],
        compiler_params=_cparams(2),
        name="mlstm_scan",
    )(qvo_t, qvo_t, qvo_t, k_tok, prep, head_g.reshape(M_HEADS, M_DV, 1))


def _mix_ffn_kernel(a_ref, ap_ref, an_ref, x_ref, xp_ref, xn_ref, wo_ref, g1_ref, g_ref, sc_ref, sh_ref,
                    gate_ref, wup_ref, cw_ref, cb_ref, wdn_ref, fg_ref, o_ref, h_scr, a_scr,
                    *, tm, final):
    i = pl.program_id(1)
    last = pl.num_programs(1) - 1
    wo = wo_ref[...]
    g1 = g1_ref[0, 0]

    def mix(a_t):
        return lax.dot_general(a_t, wo, (((0,), (0,)), ((), ())), preferred_element_type=F32)

    a_main = jnp.concatenate([a_ref[0, cc] for cc in range(a_ref.shape[1])], axis=1)
    x1 = x_ref[0] + g1 * mix(a_main)
    x1p = xp_ref[0] + g1 * mix(ap_ref[0, 0][:, LANES - HALO:])
    x1n = xn_ref[0] + g1 * mix(an_ref[0, 0][:, :HALO])

    g = g_ref[...]
    sc = sc_ref[0, 0]
    sh = sh_ref[0, 0]
    hp = jnp.where(i > 0, _norm_mod(x1p, g, sc, sh), 0.0)
    hn = jnp.where(i < last, _norm_mod(x1n, g, sc, sh), 0.0)
    h_scr[0:2 * HALO] = jnp.concatenate([hn, hp], axis=0).astype(BF16)
    h_scr[2 * HALO:] = _norm_mod(x1, g, sc, sh).astype(BF16)
    h = h_scr[...]
    ext = tm + 2 * HALO

    def conv(u, off):
        w0 = cw_ref[0, 0:1, off:off + FFN_CHUNK]
        w1 = cw_ref[0, 1:2, off:off + FFN_CHUNK]
        w2 = cw_ref[0, 2:3, off:off + FFN_CHUNK]
        prev = pltpu.roll(u, 1, axis=0)
        nxt = pltpu.roll(u, ext - 1, axis=0)
        out = w0 * prev + w1 * u + w2 * nxt + cb_ref[0, :, off:off + FFN_CHUNK]
        return out[2 * HALO:]

    for f in range(D_FF // FFN_CHUNK):
        og = f * FFN_CHUNK
        ov = D_FF + f * FFN_CHUNK
        ug = conv(jnp.dot(h, wup_ref[0, :, og:og + FFN_CHUNK], preferred_element_type=F32), og)
        uv = conv(jnp.dot(h, wup_ref[0, :, ov:ov + FFN_CHUNK], preferred_element_type=F32), ov)
        a_scr[:, og:og + FFN_CHUNK] = (ug * jax.nn.sigmoid(ug) * uv).astype(BF16)

    y = jnp.dot(a_scr[...], wdn_ref[0], preferred_element_type=F32)
    out = x1 + gate_ref[0, 0] * y
    if final:
        ms = jnp.mean(out * out, axis=-1, keepdims=True)
        out = out * lax.rsqrt(ms + EPS) * fg_ref[...]
    o_ref[0] = out


def _mix_ffn(a_t, w_o, x, g, mod4, layer, w_up, conv_w, conv_b, w_down, final_g, *, tm, chunk, final):
    bsz, s, d = x.shape
    kdim = w_o.shape[0]
    nc = s // chunk
    cpt = tm // chunk
    hb = tm // HALO
    nh = s // HALO
    return pl.pallas_call(
        functools.partial(_mix_ffn_kernel, tm=tm, final=final),
        grid=(bsz, s // tm),
        in_specs=[pl.BlockSpec((1, cpt, kdim, chunk), lambda b, i: (b, i, 0, 0)),
                  pl.BlockSpec((1, 1, kdim, LANES),
                               lambda b, i: (b, jnp.maximum(i * cpt - 1, 0), 0, chunk // LANES - 1)),
                  pl.BlockSpec((1, 1, kdim, LANES),
                               lambda b, i: (b, jnp.minimum((i + 1) * cpt, nc - 1), 0, 0)),
                  pl.BlockSpec((1, tm, d), lambda b, i: (b, i, 0)),
                  pl.BlockSpec((1, HALO, d), lambda b, i: (b, jnp.maximum(i * hb - 1, 0), 0)),
                  pl.BlockSpec((1, HALO, d), lambda b, i: (b, jnp.minimum((i + 1) * hb, nh - 1), 0)),
                  _resident((kdim, d)),
                  pl.BlockSpec((1, 1, 1, d), lambda b, i: (b, 2, 0, 0)),
                  _resident((1, d)),
                  pl.BlockSpec((1, 1, 1, d), lambda b, i: (b, 4, 0, 0)),
                  pl.BlockSpec((1, 1, 1, d), lambda b, i: (b, 3, 0, 0)),
                  pl.BlockSpec((1, 1, 1, d), lambda b, i: (b, 5, 0, 0)),
                  _resident((d, 2 * D_FF), layer),
                  _resident((3, 2 * D_FF), layer),
                  _resident((1, 2 * D_FF), layer),
                  _resident((D_FF, d), layer),
                  _resident((1, d))],
        out_specs=pl.BlockSpec((1, tm, d), lambda b, i: (b, i, 0)),
        out_shape=jax.ShapeDtypeStruct((bsz, s, d), F32),
        scratch_shapes=[pltpu.VMEM((tm + 2 * HALO, d), BF16),
                        pltpu.VMEM((tm, D_FF), BF16)],
        compiler_params=_cparams(2),
        name="mix_ffn_final" if final else "mix_ffn",
    )(a_t, a_t, a_t, x, x, x, w_o, mod4, g.reshape(1, d), mod4, mod4, mod4, w_up, conv_w,
      conv_b.reshape(DEPTH, 1, -1), w_down, final_g.reshape(1, d))


def _t5_bucket(rel):
    nb = NUM_BUCKETS // 2
    max_exact = nb // 2
    ret = jnp.where(rel > 0, nb, 0)
    n = jnp.abs(rel)
    nf = jnp.maximum(n, 1).astype(jnp.float32)
    large = max_exact + (jnp.log(nf / max_exact) / math.log(MAX_DISTANCE / max_exact)
                         * (nb - max_exact)).astype(jnp.int32)
    large = jnp.minimum(large, nb - 1)
    return ret + jnp.where(n < max_exact, n, large)


N_EDGE = 3


def _bias_kernel(bucket_ref, rb_ref, o_ref):
    hh = pl.program_id(0)
    head = (hh % A_KV) * A_GROUP + hh // A_KV
    bucket = bucket_ref[...]
    c = lax.broadcasted_iota(jnp.int32, bucket.shape, 0)
    q = lax.broadcasted_iota(jnp.int32, bucket.shape, 1)
    band = jnp.abs(c - WINDOW - q) <= WINDOW
    val = jnp.zeros(bucket.shape, F32)
    for kb in range(NUM_BUCKETS):
        val = jnp.where(bucket == kb, rb_ref[kb, head], val)
    val = val * LOG2E
    o_ref[0, 0] = jnp.where(band & (c >= WINDOW), val, -jnp.inf)
    o_ref[1, 0] = jnp.where(band, val, -jnp.inf)
    o_ref[N_EDGE - 1, 0] = jnp.where(band & (c < WINDOW + A_BLOCK), val, -jnp.inf)


def _bias_table(rel_bias):
    c_idx = jnp.arange(A_SPAN)[:, None]
    q_idx = jnp.arange(A_BLOCK)[None, :]
    bucket = _t5_bucket(c_idx - WINDOW - q_idx).astype(jnp.int32)
    return pl.pallas_call(
        _bias_kernel,
        grid=(A_HEADS,),
        in_specs=[pl.BlockSpec((A_SPAN, A_BLOCK), lambda h: (0, 0)),
                  pl.BlockSpec(memory_space=pltpu.SMEM)],
        out_specs=pl.BlockSpec((N_EDGE, 1, A_SPAN, A_BLOCK), lambda h: (0, h, 0, 0)),
        out_shape=jax.ShapeDtypeStruct((N_EDGE, A_HEADS, A_SPAN, A_BLOCK), F32),
        compiler_params=_cparams(1),
        name="bias_table",
    )(bucket, rel_bias)


def _swa_kernel(qt_ref, kp_ref, kc_ref, kn_ref, vp_ref, vc_ref, vn_ref, bias_ref, sink_ref, o_ref):
    j = pl.program_id(1)
    last = pl.num_programs(1) - 1
    n_sub = SWA_SUB
    kblk = ([kp_ref[0, 0]] + [kc_ref[0, 0, u * A_BLOCK:(u + 1) * A_BLOCK] for u in range(n_sub)]
            + [kn_ref[0, 0]])
    vblk = [vp_ref[0, 0]] + [vc_ref[0, u] for u in range(n_sub)] + [vn_ref[0, 0]]
    var = [1] * n_sub
    var[0] = jnp.where(j == 0, 0, 1)
    var[-1] = jnp.where(j == last, N_EDGE - 1, 1)
    lane_kv = lax.broadcasted_iota(jnp.int32, (A_SPAN, A_KV * A_HD), 1) // A_HD
    row_kv = lax.broadcasted_iota(jnp.int32, (A_KV * A_HD, A_SPAN), 0) // A_HD
    grp = lax.broadcasted_iota(jnp.int32, (1, A_GROUP * A_BLOCK), 1) // A_BLOCK
    groups = range(A_GROUP)
    kcat = []
    vstack = []
    qp = []
    for u in range(n_sub):
        kcat.append(jnp.concatenate(kblk[u:u + 3], axis=0))
        vcat = jnp.concatenate(vblk[u:u + 3], axis=1)
        vstack.append(jnp.concatenate([jnp.where(row_kv == kv, vcat, jnp.zeros_like(vcat))
                                       for kv in range(A_KV)], axis=1))
        qp.append(jnp.concatenate([qt_ref[0, u, g * MXU_DIM:(g + 1) * MXU_DIM, :] for g in groups],
                                  axis=1))
    ps = [[] for _ in range(n_sub)]
    rden = [[] for _ in range(n_sub)]
    for kv in range(A_KV):
        sink = jnp.zeros((1, A_GROUP * A_BLOCK), F32)
        for g in groups:
            sink = jnp.where(grp == g, sink_ref[kv * A_GROUP + g], sink)
        sink = sink * LOG2E
        for u in range(n_sub):
            bias = jnp.concatenate([bias_ref[var[u], g * A_KV + kv] for g in groups], axis=1)
            k_kv = jnp.where(lane_kv == kv, kcat[u], jnp.zeros_like(kcat[u]))
            s = jnp.dot(k_kv, qp[u], preferred_element_type=F32) + bias
            m = jnp.maximum(jnp.max(s, axis=0, keepdims=True), sink)
            p = jnp.exp2(s - m)
            rden[u].append(1.0 / (jnp.sum(p, axis=0, keepdims=True) + jnp.exp2(sink - m)))
            ps[u].append(p.astype(BF16))
    for u in range(n_sub):
        out = jnp.dot(vstack[u], jnp.concatenate(ps[u], axis=0), preferred_element_type=F32)
        out = jnp.concatenate([out[kv * A_HD:(kv + 1) * A_HD] * rden[u][kv] for kv in range(A_KV)], axis=0)
        for g in groups:
            o_ref[0, u, g * MXU_DIM:(g + 1) * MXU_DIM, :] = (
                out[:, g * A_BLOCK:(g + 1) * A_BLOCK].astype(BF16))


def _swa(k_tok, qv_t, bias_tbl, sink):
    bsz, _, s, _ = k_tok.shape
    nb = s // A_BLOCK
    ns = SWA_SUB
    assert nb % ns == 0 and nb >= 2 * ns
    qw = A_HEADS * A_HD
    vrow = qw // MXU_DIM
    prev = lambda j: jnp.maximum(ns * j - 1, 0)
    nxt = lambda j: jnp.minimum(ns * j + ns, nb - 1)
    k_edge = lambda row: pl.BlockSpec((1, 1, A_BLOCK, MXU_DIM), lambda b, j: (b, 0, row(j), 0))
    v_edge = lambda row: pl.BlockSpec((1, 1, MXU_DIM, A_BLOCK), lambda b, j: (b, row(j), vrow, 0))
    return pl.pallas_call(
        _swa_kernel,
        grid=(bsz, nb // ns),
        in_specs=[pl.BlockSpec((1, ns, qw, A_BLOCK), lambda b, j: (b, j, 0, 0)),
                  k_edge(prev), pl.BlockSpec((1, 1, ns * A_BLOCK, MXU_DIM), lambda b, j: (b, 0, j, 0)),
                  k_edge(nxt),
                  v_edge(prev), pl.BlockSpec((1, ns, MXU_DIM, A_BLOCK), lambda b, j: (b, j, vrow, 0)),
                  v_edge(nxt),
                  _resident((N_EDGE, A_HEADS, A_SPAN, A_BLOCK)),
                  pl.BlockSpec(memory_space=pltpu.SMEM)],
        out_specs=pl.BlockSpec((1, ns, qw, A_BLOCK), lambda b, j: (b, j, 0, 0)),
        out_shape=jax.ShapeDtypeStruct((bsz, nb, qw, A_BLOCK), BF16),
        compiler_params=_cparams(2),
        name="swa",
    )(qv_t, k_tok, k_tok, k_tok, qv_t, qv_t, qv_t, bias_tbl, sink)


def _prep_weights(mlstm_w_in, mlstm_b_gate, mlstm_w_out, attn_w_in, attn_w_out, ffn_w_up, ffn_w_down):
    qk = M_HEADS * M_DK
    w_m = mlstm_w_in[0]
    w_q = w_m[:, :qk] * (M_DK ** -0.5)
    perm = lambda g: g.reshape(-1, 2, 2, M_HEADS).swapaxes(1, 2).reshape(-1, N_GATES)
    w_t = jnp.concatenate([w_q, w_m[:, 2 * qk:4 * qk], perm(w_m[:, 4 * qk:])], axis=1).T.astype(BF16)
    w_k = w_m[:, qk:2 * qk].astype(BF16)
    b_gate = jnp.broadcast_to(perm(mlstm_b_gate[0][None]).T, (N_GATES, SCAN_CHUNK))
    qw = A_HEADS * A_HD
    kw = A_KV * A_HD
    w_a = attn_w_in[0]
    wq = w_a[:, :qw].reshape(D_MODEL, A_KV, A_GROUP, A_HD).transpose(0, 2, 1, 3).reshape(D_MODEL, qw)
    w_attn_k = w_a[:, qw:qw + kw].astype(BF16)
    w_attn_t = jnp.concatenate([wq * (A_HD ** -0.5 * LOG2E), w_a[:, qw + kw:]], axis=1).T.astype(BF16)
    wo_attn = attn_w_out[0].reshape(A_KV, A_GROUP, A_HD, D_MODEL).transpose(1, 0, 2, 3)
    wo_attn = wo_attn.reshape(qw, D_MODEL).astype(BF16)
    return dict(w_k=w_k, w_t=w_t, b_gate=b_gate,
                wo_m=mlstm_w_out[0].astype(BF16), w_attn_k=w_attn_k, w_attn_t=w_attn_t, wo_attn=wo_attn,
                w_up=ffn_w_up.astype(BF16), w_down=ffn_w_down.astype(BF16))


def _trunk(x, mod, wts, bias_tbl, norm_g, mlstm_head_g, attn_sink, ffn_conv_w, ffn_conv_b, final_g,
           *, tm, tm_ffn):
    bsz, s, d = x.shape
    mod4 = mod[0].reshape(bsz, 6, 1, d)
    k_tok, qvo_t, gates = _inproj_split(x, norm_g[0, 0], mod4, wts["w_k"], wts["w_t"],
                                        tm=tm, chunk=SCAN_CHUNK, tn=512, n_extra=N_GATES)
    prep = _gateprep(gates, wts["b_gate"], SCAN_CHUNK)
    a_t = _mlstm_scan(k_tok, qvo_t, prep, mlstm_head_g[0], SCAN_CHUNK)
    x = _mix_ffn(a_t, wts["wo_m"], x, norm_g[0, 1], mod4, 0, wts["w_up"], ffn_conv_w, ffn_conv_b,
                 wts["w_down"], final_g, tm=tm_ffn, chunk=SCAN_CHUNK, final=False)
    mod4 = mod[1].reshape(bsz, 6, 1, d)
    k_tok, qv_t = _inproj_split(x, norm_g[1, 0], mod4, wts["w_attn_k"], wts["w_attn_t"],
                                tm=tm, chunk=A_BLOCK, tn=640)
    a_t = _swa(k_tok, qv_t, bias_tbl, attn_sink[0])
    return _mix_ffn(a_t, wts["wo_attn"], x, norm_g[1, 1], mod4, 1, wts["w_up"], ffn_conv_w,
                    ffn_conv_b, wts["w_down"], final_g, tm=tm_ffn, chunk=A_BLOCK, final=True)


def kernel(x_prompt, x_sample, c_prompt, c_sample, adaln_w, adaln_b, norm_g, mlstm_w_in, mlstm_b_gate, mlstm_head_g, mlstm_w_out, attn_w_in, attn_sink, attn_w_out, rel_bias, ffn_w_up, ffn_conv_w, ffn_conv_b, ffn_w_down, final_g):
    nbp = x_prompt.shape[0]
    mod = _adaln(jnp.concatenate([c_prompt, c_sample], axis=0), adaln_w, adaln_b)
    wts = _prep_weights(mlstm_w_in, mlstm_b_gate, mlstm_w_out, attn_w_in, attn_w_out, ffn_w_up, ffn_w_down)
    bias_tbl = _bias_table(rel_bias)
    run = functools.partial(_trunk, wts=wts, bias_tbl=bias_tbl, norm_g=norm_g,
                            mlstm_head_g=mlstm_head_g, attn_sink=attn_sink, ffn_conv_w=ffn_conv_w,
                            ffn_conv_b=ffn_conv_b, final_g=final_g, tm=1024, tm_ffn=1024)
    return run(x_prompt, mod[:, :nbp]), run(x_sample, mod[:, nbp:])
```

```python
import functools
import math

import jax
import jax.numpy as jnp
from jax import lax
from jax.experimental import pallas as pl
from jax.experimental.pallas import tpu as pltpu

F32 = jnp.float32
BF16 = jnp.bfloat16

D_MODEL = 1024
DEPTH = 2
M_HEADS = 4
M_DK = 256
M_DV = 256
A_HEADS = 16
A_KV = 4
A_GROUP = A_HEADS // A_KV
A_HD = 64
WINDOW = 128
A_BLOCK = 128
A_SPAN = A_BLOCK + 2 * WINDOW
NUM_BUCKETS = 32
MAX_DISTANCE = 128
D_FF = 2816
EPS = 1e-6
LOG2E = 1.4426950408889634

LANES = 128
MXU_DIM = 256
VMEM_LIMIT = 56 * 1024 * 1024

SCAN_CHUNK = 256
N_GATES = 4 * M_HEADS
HALO = 8
FFN_CHUNK = MXU_DIM
SWA_SUB = 8


def _cparams(n_axes):
    return pltpu.CompilerParams(dimension_semantics=("parallel",) * n_axes,
                                vmem_limit_bytes=VMEM_LIMIT)


def _resident(shape, layer=None):
    nd = len(shape)
    if layer is None:
        return pl.BlockSpec(shape, lambda *_: (0,) * nd, pipeline_mode=pl.Buffered(1))
    return pl.BlockSpec((1,) + tuple(shape), lambda *_: (layer,) + (0,) * nd,
                        pipeline_mode=pl.Buffered(1))


def _norm_mod(x, g, sc, sh):
    ms = jnp.mean(x * x, axis=-1, keepdims=True)
    return x * lax.rsqrt(ms + EPS) * (g * (1.0 + sc)) + sh


def _adaln_kernel(c_ref, w_ref, b_ref, o_ref):
    c = c_ref[...]
    s = c * jax.nn.sigmoid(c)
    o_ref[0] = jnp.dot(s.astype(BF16), w_ref[0].astype(BF16), preferred_element_type=F32) + b_ref[0]


def _adaln(c_all, adaln_w, adaln_b):
    nb = c_all.shape[0]
    n = adaln_w.shape[-1]
    tn = 1536
    return pl.pallas_call(
        _adaln_kernel,
        grid=(DEPTH, n // tn),
        in_specs=[pl.BlockSpec((nb, D_MODEL), lambda l, j: (0, 0)),
                  pl.BlockSpec((1, D_MODEL, tn), lambda l, j: (l, 0, j)),
                  pl.BlockSpec((1, 1, tn), lambda l, j: (l, 0, j))],
        out_specs=pl.BlockSpec((1, nb, tn), lambda l, j: (l, 0, j)),
        out_shape=jax.ShapeDtypeStruct((DEPTH, nb, n), F32),
        compiler_params=_cparams(2),
        name="adaln",
    )(c_all, adaln_w, adaln_b.reshape(DEPTH, 1, n))


def _inproj_split_kernel(x_ref, g_ref, sc_ref, sh_ref, wk_ref, wt_ref, k_ref, t_ref, *rest,
                         tn, chunk, n_extra):
    h = _norm_mod(x_ref[0], g_ref[...], sc_ref[0, 0], sh_ref[0, 0]).astype(BF16)
    tm = h.shape[0]
    nk = wk_ref.shape[1]
    tk = min(tn, nk)
    hpt = tk // MXU_DIM
    for j in range(nk // tk):
        res = jnp.dot(h, wk_ref[:, j * tk:(j + 1) * tk], preferred_element_type=F32).astype(BF16)
        for hh in range(hpt):
            k_ref[0, j * hpt + hh] = res[:, hh * MXU_DIM:(hh + 1) * MXU_DIM]
    nt = wt_ref.shape[0] - n_extra
    for j in range(nt // tn):
        hi = (j + 1) * tn + (n_extra if j == nt // tn - 1 else 0)
        res = lax.dot_general(wt_ref[j * tn:hi, :], h, (((1,), (1,)), ((), ())),
                              preferred_element_type=F32)
        for cc in range(tm // chunk):
            cols = slice(cc * chunk, (cc + 1) * chunk)
            t_ref[0, cc, j * tn:(j + 1) * tn, :] = res[:tn, cols].astype(BF16)
            if hi > (j + 1) * tn:
                rest[0][0, cc] = res[tn:, cols]


def _inproj_split(x, g, mod4, w_k, w_t, *, tm, chunk, tn, n_extra=0):
    bsz, s, d = x.shape
    nk = w_k.shape[1]
    nt = w_t.shape[0] - n_extra
    out_specs = [pl.BlockSpec((1, nk // MXU_DIM, tm, MXU_DIM), lambda b, i: (b, 0, i, 0)),
                 pl.BlockSpec((1, tm // chunk, nt, chunk), lambda b, i: (b, i, 0, 0))]
    out_shape = [jax.ShapeDtypeStruct((bsz, nk // MXU_DIM, s, MXU_DIM), BF16),
                 jax.ShapeDtypeStruct((bsz, s // chunk, nt, chunk), BF16)]
    if n_extra:
        out_specs.append(pl.BlockSpec((1, tm // chunk, n_extra, chunk), lambda b, i: (b, i, 0, 0)))
        out_shape.append(jax.ShapeDtypeStruct((bsz, s // chunk, n_extra, chunk), F32))
    return pl.pallas_call(
        functools.partial(_inproj_split_kernel, tn=tn, chunk=chunk, n_extra=n_extra),
        grid=(bsz, s // tm),
        in_specs=[pl.BlockSpec((1, tm, d), lambda b, i: (b, i, 0)),
                  _resident((1, d)),
                  pl.BlockSpec((1, 1, 1, d), lambda b, i: (b, 1, 0, 0)),
                  pl.BlockSpec((1, 1, 1, d), lambda b, i: (b, 0, 0, 0)),
                  _resident((d, nk)), _resident((nt + n_extra, d))],
        out_specs=out_specs, out_shape=out_shape,
        compiler_params=_cparams(2),
        name="inproj_m" if n_extra else "inproj_a",
    )(x, g.reshape(1, d), mod4, mod4, w_k, w_t)


N_PREP = 6


def _gateprep_kernel(g_ref, b_ref, p_ref, *, nc, chunk):
    s = nc * chunk
    gt = jnp.concatenate([g_ref[0, c] + b_ref[...] for c in range(nc)], axis=1)
    ig = gt[0:8]
    fg = gt[8:16]
    row = lax.broadcasted_iota(jnp.int32, (8, s), 0)
    pos = lax.broadcasted_iota(jnp.int32, (8, s), 1) & (chunk - 1)
    fwd = row < M_HEADS

    def seg_scan(x, op, ident):
        k = 1
        while k < chunk:
            sh_f = jnp.where(pos >= k, pltpu.roll(x, k, axis=1), ident)
            sh_b = jnp.where(pos < chunk - k, pltpu.roll(x, s - k, axis=1), ident)
            x = op(x, jnp.where(fwd, sh_f, sh_b))
            k *= 2
        return x

    lf = jnp.minimum(fg, 0.0) - jnp.log1p(jnp.exp(-jnp.abs(fg)))
    bsum = seg_scan(lf, jnp.add, 0.0)
    beta = ig - bsum
    cmax = seg_scan(beta, jnp.maximum, -jnp.inf)

    fwd1 = fwd[:, :1]
    g_max = [jnp.max(beta[:, c * chunk:(c + 1) * chunk], axis=1, keepdims=True) for c in range(nc)]
    b_last = [jnp.where(fwd1, bsum[:, (c + 1) * chunk - 1:(c + 1) * chunk], bsum[:, c * chunk:c * chunk + 1])
              for c in range(nc)]
    m_f = [None] * nc
    m_b = [None] * nc
    mf = jnp.zeros((8, 1), F32)
    mb = jnp.zeros((8, 1), F32)
    for i in range(nc):
        j = nc - 1 - i
        m_f[i] = mf
        m_b[j] = mb
        mf = b_last[i] + jnp.maximum(mf, g_max[i])
        mb = b_last[j] + jnp.maximum(mb, g_max[j])
    wide = lambda v: jnp.broadcast_to(v, (8, chunk))
    m_in = jnp.concatenate([wide(jnp.where(fwd1, m_f[c], m_b[c])) for c in range(nc)], axis=1)
    g_all = jnp.concatenate([wide(g_max[c]) for c in range(nc)], axis=1)

    mx = jnp.maximum(cmax, m_in)
    m_c = jnp.maximum(m_in, g_all)
    quantities = (beta * LOG2E, mx * LOG2E, jnp.exp(m_in - mx), jnp.exp(-(bsum + mx)), jnp.exp(beta - m_c),
                  jnp.exp(m_in - m_c))
    row16 = lax.broadcasted_iota(jnp.int32, (16, s), 0)
    for h in range(M_HEADS):
        tile = jnp.zeros((16, s), F32)
        for d in range(2):
            for qi, val in enumerate(quantities):
                src = val[d * M_HEADS + h:d * M_HEADS + h + 1]
                tile = jnp.where(row16 == d * 8 + qi, jnp.broadcast_to(src, (16, s)), tile)
        for c in range(nc):
            p_ref[0, c, h] = tile[:, c * chunk:(c + 1) * chunk]


def _gateprep(gates, bias, chunk):
    bsz, nc, ng, _ = gates.shape
    return pl.pallas_call(
        functools.partial(_gateprep_kernel, nc=nc, chunk=chunk),
        grid=(bsz,),
        in_specs=[pl.BlockSpec((1, nc, ng, chunk), lambda b: (b, 0, 0, 0)),
                  _resident((ng, chunk))],
        out_specs=pl.BlockSpec((1, nc, M_HEADS, 16, chunk), lambda b: (b, 0, 0, 0, 0)),
        out_shape=jax.ShapeDtypeStruct((bsz, nc, M_HEADS, 16, chunk), F32),
        compiler_params=_cparams(1),
        name="gateprep",
    )(gates, bias)


def _mlstm_kernel(qt_ref, vt_ref, ot_ref, k_ref, p_ref, hg_ref, at_ref, acc, ct_scr, n_scr,
                  *, nc, chunk):
    L = chunk
    ct_scr[...] = jnp.zeros_like(ct_scr)
    n_scr[...] = jnp.zeros_like(n_scr)
    s_idx = lax.broadcasted_iota(jnp.int32, (L, L), 0)
    t_idx = lax.broadcasted_iota(jnp.int32, (L, L), 1)
    tri = (s_idx <= t_idx, s_idx >= t_idx)
    n_pad = 16

    def body(i, carry, *, finish):
        cs = (i, nc - 1 - i)
        qt = [qt_ref[0, c] for c in cs]
        vt = [vt_ref[0, c] for c in cs]
        k = [k_ref[0, 0, pl.ds(pl.multiple_of(c * L, L), L), :] for c in cs]
        prm = [p_ref[0, cs[d], 0, d * 8:(d + 1) * 8, :] for d in range(2)]
        ct_old = [ct_scr[d] for d in range(2)]
        n_old = [n_scr[d] for d in range(2)]
        r1 = []
        for d in range(2):
            lhs = jnp.concatenate([k[d], ct_old[d].astype(BF16),
                                   jnp.concatenate([n_old[d], n_old[d]], axis=0).astype(BF16)], axis=0)
            r1.append(jnp.dot(lhs, qt[d], preferred_element_type=F32))
        r2 = []
        for d in range(2):
            w = prm[d][4:5]
            vts = (vt[d].astype(F32) * w).astype(BF16)
            lhs2 = jnp.concatenate([vts, jnp.broadcast_to(w, (n_pad, L)).astype(BF16)], axis=0)
            r2.append(jnp.dot(lhs2, k[d], preferred_element_type=F32))
        for d in range(2):
            beta, mx, w_inter, neg_mt, _, decay = (prm[d][j:j + 1] for j in range(N_PREP))
            d_t = jnp.where(tri[d], jnp.exp2(jnp.broadcast_to(beta, (L, L)).T - mx), 0.0)
            p_t = r1[d][0:L] * d_t
            inter = r1[d][L:L + M_DV]
            qn = r1[d][L + M_DV:L + M_DV + 1]
            num = jnp.dot(vt[d], p_t.astype(BF16), preferred_element_type=F32) + w_inter * inter
            den = jnp.sum(p_t, axis=0, keepdims=True) + w_inter * qn
            hout = num * (1.0 / jnp.maximum(jnp.abs(den), neg_mt))
            if finish:
                hs = acc[cs[d]] + hout
                ms = jnp.mean(hs * hs, axis=0, keepdims=True)
                y = hs * lax.rsqrt(ms + EPS) * hg_ref[0]
                gate = jax.nn.sigmoid(ot_ref[0, cs[d]].astype(F32))
                at_ref[0, cs[d]] = (y * gate).astype(BF16)
            else:
                acc[cs[d]] = hout
            ct_scr[d] = decay * ct_old[d] + r2[d][0:M_DV]
            n_scr[d] = decay * n_old[d] + r2[d][M_DV:M_DV + 8]
        return carry

    half = nc // 2
    lax.fori_loop(0, half, functools.partial(body, finish=False), 0, unroll=4)
    lax.fori_loop(half, nc, functools.partial(body, finish=True), 0, unroll=2)


def _mlstm_scan(k_tok, qvo_t, prep, head_g, chunk):
    bsz, _, s, _ = k_tok.shape
    nc = s // chunk
    assert nc % 8 == 0
    blk_t = lambda off: pl.BlockSpec((1, nc, M_DK, chunk), lambda b, h: (b, 0, off + h, 0))
    return pl.pallas_call(
        functools.partial(_mlstm_kernel, nc=nc, chunk=chunk),
        grid=(bsz, M_HEADS),
        in_specs=[blk_t(0), blk_t(M_HEADS), blk_t(2 * M_HEADS),
                  pl.BlockSpec((1, 1, s, M_DK), lambda b, h: (b, h, 0, 0)),
                  pl.BlockSpec((1, nc, 1, 16, chunk), lambda b, h: (b, 0, h, 0, 0)),
                  pl.BlockSpec((1, M_DV, 1), lambda b, h: (h, 0, 0))],
        out_specs=pl.BlockSpec((1, nc, M_DV, chunk), lambda b, h: (b, 0, h, 0)),
        out_shape=jax.ShapeDtypeStruct((bsz, nc, M_HEADS * M_DV, chunk), BF16),
        scratch_shapes=[pltpu.VMEM((nc, M_DV, chunk), F32),
                        pltpu.VMEM((2, M_DV, M_DK), F32),
                        pltpu.VMEM((2, 8, M_DK), F32)],
        compiler_params=_cparams(2),
        name="mlstm_scan",
    )(qvo_t, qvo_t, qvo_t, k_tok, prep, head_g.reshape(M_HEADS, M_DV, 1))


def _mix_ffn_kernel(a_ref, ap_ref, an_ref, x_ref, xp_ref, xn_ref, wo_ref, g1_ref, g_ref, sc_ref, sh_ref,
                    gate_ref, wup_ref, cw_ref, cb_ref, wdn_ref, fg_ref, o_ref, h_scr, a_scr,
                    *, tm, final):
    i = pl.program_id(1)
    last = pl.num_programs(1) - 1
    wo = wo_ref[...]
    g1 = g1_ref[0, 0]

    def mix(a_t):
        return lax.dot_general(a_t, wo, (((0,), (0,)), ((), ())), preferred_element_type=F32)

    a_main = jnp.concatenate([a_ref[0, cc] for cc in range(a_ref.shape[1])], axis=1)
    x1 = x_ref[0] + g1 * mix(a_main)
    x1p = xp_ref[0] + g1 * mix(ap_ref[0, 0][:, LANES - HALO:])
    x1n = xn_ref[0] + g1 * mix(an_ref[0, 0][:, :HALO])

    g = g_ref[...]
    sc = sc_ref[0, 0]
    sh = sh_ref[0, 0]
    hp = jnp.where(i > 0, _norm_mod(x1p, g, sc, sh), 0.0)
    hn = jnp.where(i < last, _norm_mod(x1n, g, sc, sh), 0.0)
    h_scr[0:2 * HALO] = jnp.concatenate([hn, hp], axis=0).astype(BF16)
    h_scr[2 * HALO:] = _norm_mod(x1, g, sc, sh).astype(BF16)
    h = h_scr[...]
    ext = tm + 2 * HALO

    def conv(u, off):
        w0 = cw_ref[0, 0:1, off:off + FFN_CHUNK]
        w1 = cw_ref[0, 1:2, off:off + FFN_CHUNK]
        w2 = cw_ref[0, 2:3, off:off + FFN_CHUNK]
        prev = pltpu.roll(u, 1, axis=0)
        nxt = pltpu.roll(u, ext - 1, axis=0)
        out = w0 * prev + w1 * u + w2 * nxt + cb_ref[0, :, off:off + FFN_CHUNK]
        return out[2 * HALO:]

    for f in range(D_FF // FFN_CHUNK):
        og = f * FFN_CHUNK
        ov = D_FF + f * FFN_CHUNK
        ug = conv(jnp.dot(h, wup_ref[0, :, og:og + FFN_CHUNK], preferred_element_type=F32), og)
        uv = conv(jnp.dot(h, wup_ref[0, :, ov:ov + FFN_CHUNK], preferred_element_type=F32), ov)
        a_scr[:, og:og + FFN_CHUNK] = (ug * jax.nn.sigmoid(ug) * uv).astype(BF16)

    y = jnp.dot(a_scr[...], wdn_ref[0], preferred_element_type=F32)
    out = x1 + gate_ref[0, 0] * y
    if final:
        ms = jnp.mean(out * out, axis=-1, keepdims=True)
        out = out * lax.rsqrt(ms + EPS) * fg_ref[...]
    o_ref[0] = out


def _mix_ffn(a_t, w_o, x, g, mod4, layer, w_up, conv_w, conv_b, w_down, final_g, *, tm, chunk, final):
    bsz, s, d = x.shape
    kdim = w_o.shape[0]
    nc = s // chunk
    cpt = tm // chunk
    hb = tm // HALO
    nh = s // HALO
    return pl.pallas_call(
        functools.partial(_mix_ffn_kernel, tm=tm, final=final),
        grid=(bsz, s // tm),
        in_specs=[pl.BlockSpec((1, cpt, kdim, chunk), lambda b, i: (b, i, 0, 0)),
                  pl.BlockSpec((1, 1, kdim, LANES),
                               lambda b, i: (b, jnp.maximum(i * cpt - 1, 0), 0, chunk // LANES - 1)),
                  pl.BlockSpec((1, 1, kdim, LANES),
                               lambda b, i: (b, jnp.minimum((i + 1) * cpt, nc - 1), 0, 0)),
                  pl.BlockSpec((1, tm, d), lambda b, i: (b, i, 0)),
                  pl.BlockSpec((1, HALO, d), lambda b, i: (b, jnp.maximum(i * hb - 1, 0), 0)),
                  pl.BlockSpec((1, HALO, d), lambda b, i: (b, jnp.minimum((i + 1) * hb, nh - 1), 0)),
                  _resident((kdim, d)),
                  pl.BlockSpec((1, 1, 1, d), lambda b, i: (b, 2, 0, 0)),
                  _resident((1, d)),
                  pl.BlockSpec((1, 1, 1, d), lambda b, i: (b, 4, 0, 0)),
                  pl.BlockSpec((1, 1, 1, d), lambda b, i: (b, 3, 0, 0)),
                  pl.BlockSpec((1, 1, 1, d), lambda b, i: (b, 5, 0, 0)),
                  _resident((d, 2 * D_FF), layer),
                  _resident((3, 2 * D_FF), layer),
                  _resident((1, 2 * D_FF), layer),
                  _resident((D_FF, d), layer),
                  _resident((1, d))],
        out_specs=pl.BlockSpec((1, tm, d), lambda b, i: (b, i, 0)),
        out_shape=jax.ShapeDtypeStruct((bsz, s, d), F32),
        scratch_shapes=[pltpu.VMEM((tm + 2 * HALO, d), BF16),
                        pltpu.VMEM((tm, D_FF), BF16)],
        compiler_params=_cparams(2),
        name="mix_ffn_final" if final else "mix_ffn",
    )(a_t, a_t, a_t, x, x, x, w_o, mod4, g.reshape(1, d), mod4, mod4, mod4, w_up, conv_w,
      conv_b.reshape(DEPTH, 1, -1), w_down, final_g.reshape(1, d))


def _t5_bucket(rel):
    nb = NUM_BUCKETS // 2
    max_exact = nb // 2
    ret = jnp.where(rel > 0, nb, 0)
    n = jnp.abs(rel)
    nf = jnp.maximum(n, 1).astype(jnp.float32)
    large = max_exact + (jnp.log(nf / max_exact) / math.log(MAX_DISTANCE / max_exact)
                         * (nb - max_exact)).astype(jnp.int32)
    large = jnp.minimum(large, nb - 1)
    return ret + jnp.where(n < max_exact, n, large)


N_EDGE = 3


def _bias_kernel(bucket_ref, rb_ref, o_ref):
    hh = pl.program_id(0)
    head = (hh % A_KV) * A_GROUP + hh // A_KV
    bucket = bucket_ref[...]
    c = lax.broadcasted_iota(jnp.int32, bucket.shape, 0)
    q = lax.broadcasted_iota(jnp.int32, bucket.shape, 1)
    band = jnp.abs(c - WINDOW - q) <= WINDOW
    val = jnp.zeros(bucket.shape, F32)
    for kb in range(NUM_BUCKETS):
        val = jnp.where(bucket == kb, rb_ref[kb, head], val)
    val = val * LOG2E
    o_ref[0, 0] = jnp.where(band & (c >= WINDOW), val, -jnp.inf)
    o_ref[1, 0] = jnp.where(band, val, -jnp.inf)
    o_ref[N_EDGE - 1, 0] = jnp.where(band & (c < WINDOW + A_BLOCK), val, -jnp.inf)


def _bias_table(rel_bias):
    c_idx = jnp.arange(A_SPAN)[:, None]
    q_idx = jnp.arange(A_BLOCK)[None, :]
    bucket = _t5_bucket(c_idx - WINDOW - q_idx).astype(jnp.int32)
    return pl.pallas_call(
        _bias_kernel,
        grid=(A_HEADS,),
        in_specs=[pl.BlockSpec((A_SPAN, A_BLOCK), lambda h: (0, 0)),
                  pl.BlockSpec(memory_space=pltpu.SMEM)],
        out_specs=pl.BlockSpec((N_EDGE, 1, A_SPAN, A_BLOCK), lambda h: (0, h, 0, 0)),
        out_shape=jax.ShapeDtypeStruct((N_EDGE, A_HEADS, A_SPAN, A_BLOCK), F32),
        compiler_params=_cparams(1),
        name="bias_table",
    )(bucket, rel_bias)


def _swa_kernel(qt_ref, kp_ref, kc_ref, kn_ref, vp_ref, vc_ref, vn_ref, bias_ref, sink_ref, o_ref):
    j = pl.program_id(1)
    last = pl.num_programs(1) - 1
    n_sub = SWA_SUB
    kblk = ([kp_ref[0, 0]] + [kc_ref[0, 0, u * A_BLOCK:(u + 1) * A_BLOCK] for u in range(n_sub)]
            + [kn_ref[0, 0]])
    vblk = [vp_ref[0, 0]] + [vc_ref[0, u] for u in range(n_sub)] + [vn_ref[0, 0]]
    var = [1] * n_sub
    var[0] = jnp.where(j == 0, 0, 1)
    var[-1] = jnp.where(j == last, N_EDGE - 1, 1)
    lane_kv = lax.broadcasted_iota(jnp.int32, (A_SPAN, A_KV * A_HD), 1) // A_HD
    row_kv = lax.broadcasted_iota(jnp.int32, (A_KV * A_HD, A_SPAN), 0) // A_HD
    grp = lax.broadcasted_iota(jnp.int32, (1, A_GROUP * A_BLOCK), 1) // A_BLOCK
    groups = range(A_GROUP)
    kcat = []
    vstack = []
    qp = []
    for u in range(n_sub):
        kcat.append(jnp.concatenate(kblk[u:u + 3], axis=0))
        vcat = jnp.concatenate(vblk[u:u + 3], axis=1)
        vstack.append(jnp.concatenate([jnp.where(row_kv == kv, vcat, jnp.zeros_like(vcat))
                                       for kv in range(A_KV)], axis=1))
        qp.append(jnp.concatenate([qt_ref[0, u, g * MXU_DIM:(g + 1) * MXU_DIM, :] for g in groups],
                                  axis=1))
    ps = [[] for _ in range(n_sub)]
    rden = [[] for _ in range(n_sub)]
    for kv in range(A_KV):
        sink = jnp.zeros((1, A_GROUP * A_BLOCK), F32)
        for g in groups:
            sink = jnp.where(grp == g, sink_ref[kv * A_GROUP + g], sink)
        sink = sink * LOG2E
        for u in range(n_sub):
            bias = jnp.concatenate([bias_ref[var[u], g * A_KV + kv] for g in groups], axis=1)
            k_kv = jnp.where(lane_kv == kv, kcat[u], jnp.zeros_like(kcat[u]))
            s = jnp.dot(k_kv, qp[u], preferred_element_type=F32) + bias
            m = jnp.maximum(jnp.max(s, axis=0, keepdims=True), sink)
            p = jnp.exp2(s - m)
            rden[u].append(1.0 / (jnp.sum(p, axis=0, keepdims=True) + jnp.exp2(sink - m)))
            ps[u].append(p.astype(BF16))
    for u in range(n_sub):
        out = jnp.dot(vstack[u], jnp.concatenate(ps[u], axis=0), preferred_element_type=F32)
        out = jnp.concatenate([out[kv * A_HD:(kv + 1) * A_HD] * rden[u][kv] for kv in range(A_KV)], axis=0)
        for g in groups:
            o_ref[0, u, g * MXU_DIM:(g + 1) * MXU_DIM, :] = (
                out[:, g * A_BLOCK:(g + 1) * A_BLOCK].astype(BF16))


def _swa(k_tok, qv_t, bias_tbl, sink):
    bsz, _, s, _ = k_tok.shape
    nb = s // A_BLOCK
    ns = SWA_SUB
    assert nb % ns == 0 and nb >= 2 * ns
    qw = A_HEADS * A_HD
    vrow = qw // MXU_DIM
    prev = lambda j: jnp.maximum(ns * j - 1, 0)
    nxt = lambda j: jnp.minimum(ns * j + ns, nb - 1)
    k_edge = lambda row: pl.BlockSpec((1, 1, A_BLOCK, MXU_DIM), lambda b, j: (b, 0, row(j), 0))
    v_edge = lambda row: pl.BlockSpec((1, 1, MXU_DIM, A_BLOCK), lambda b, j: (b, row(j), vrow, 0))
    return pl.pallas_call(
        _swa_kernel,
        grid=(bsz, nb // ns),
        in_specs=[pl.BlockSpec((1, ns, qw, A_BLOCK), lambda b, j: (b, j, 0, 0)),
                  k_edge(prev), pl.BlockSpec((1, 1, ns * A_BLOCK, MXU_DIM), lambda b, j: (b, 0, j, 0)),
                  k_edge(nxt),
                  v_edge(prev), pl.BlockSpec((1, ns, MXU_DIM, A_BLOCK), lambda b, j: (b, j, vrow, 0)),
                  v_edge(nxt),
                  _resident((N_EDGE, A_HEADS, A_SPAN, A_BLOCK)),
                  pl.BlockSpec(memory_space=pltpu.SMEM)],
        out_specs=pl.BlockSpec((1, ns, qw, A_BLOCK), lambda b, j: (b, j, 0, 0)),
        out_shape=jax.ShapeDtypeStruct((bsz, nb, qw, A_BLOCK), BF16),
        compiler_params=_cparams(2),
        name="swa",
    )(qv_t, k_tok, k_tok, k_tok, qv_t, qv_t, qv_t, bias_tbl, sink)


def _prep_weights(mlstm_w_in, mlstm_b_gate, mlstm_w_out, attn_w_in, attn_w_out, ffn_w_up, ffn_w_down):
    qk = M_HEADS * M_DK
    w_m = mlstm_w_in[0]
    w_q = w_m[:, :qk] * (M_DK ** -0.5)
    perm = lambda g: g.reshape(-1, 2, 2, M_HEADS).swapaxes(1, 2).reshape(-1, N_GATES)
    w_t = jnp.concatenate([w_q, w_m[:, 2 * qk:4 * qk], perm(w_m[:, 4 * qk:])], axis=1).T.astype(BF16)
    w_k = w_m[:, qk:2 * qk].astype(BF16)
    b_gate = jnp.broadcast_to(perm(mlstm_b_gate[0][None]).T, (N_GATES, SCAN_CHUNK))
    qw = A_HEADS * A_HD
    kw = A_KV * A_HD
    w_a = attn_w_in[0]
    wq = w_a[:, :qw].reshape(D_MODEL, A_KV, A_GROUP, A_HD).transpose(0, 2, 1, 3).reshape(D_MODEL, qw)
    w_attn_k = w_a[:, qw:qw + kw].astype(BF16)
    w_attn_t = jnp.concatenate([wq * (A_HD ** -0.5 * LOG2E), w_a[:, qw + kw:]], axis=1).T.astype(BF16)
    wo_attn = attn_w_out[0].reshape(A_KV, A_GROUP, A_HD, D_MODEL).transpose(1, 0, 2, 3)
    wo_attn = wo_attn.reshape(qw, D_MODEL).astype(BF16)
    return dict(w_k=w_k, w_t=w_t, b_gate=b_gate,
                wo_m=mlstm_w_out[0].astype(BF16), w_attn_k=w_attn_k, w_attn_t=w_attn_t, wo_attn=wo_attn,
                w_up=ffn_w_up.astype(BF16), w_down=ffn_w_down.astype(BF16))


def _trunk(x, mod, wts, bias_tbl, norm_g, mlstm_head_g, attn_sink, ffn_conv_w, ffn_conv_b, final_g,
           *, tm, tm_ffn):
    bsz, s, d = x.shape
    mod4 = mod[0].reshape(bsz, 6, 1, d)
    k_tok, qvo_t, gates = _inproj_split(x, norm_g[0, 0], mod4, wts["w_k"], wts["w_t"],
                                        tm=tm, chunk=SCAN_CHUNK, tn=512, n_extra=N_GATES)
    prep = _gateprep(gates, wts["b_gate"], SCAN_CHUNK)
    a_t = _mlstm_scan(k_tok, qvo_t, prep, mlstm_head_g[0], SCAN_CHUNK)
    x = _mix_ffn(a_t, wts["wo_m"], x, norm_g[0, 1], mod4, 0, wts["w_up"], ffn_conv_w, ffn_conv_b,
                 wts["w_down"], final_g, tm=tm_ffn, chunk=SCAN_CHUNK, final=False)
    mod4 = mod[1].reshape(bsz, 6, 1, d)
    k_tok, qv_t = _inproj_split(x, norm_g[1, 0], mod4, wts["w_attn_k"], wts["w_attn_t"],
                                tm=tm, chunk=A_BLOCK, tn=640)
    a_t = _swa(k_tok, qv_t, bias_tbl, attn_sink[0])
    return _mix_ffn(a_t, wts["wo_attn"], x, norm_g[1, 1], mod4, 1, wts["w_up"], ffn_conv_w,
                    ffn_conv_b, wts["w_down"], final_g, tm=tm_ffn, chunk=A_BLOCK, final=True)


def kernel(x_prompt, x_sample, c_prompt, c_sample, adaln_w, adaln_b, norm_g, mlstm_w_in, mlstm_b_gate, mlstm_head_g, mlstm_w_out, attn_w_in, attn_sink, attn_w_out, rel_bias, ffn_w_up, ffn_conv_w, ffn_conv_b, ffn_w_down, final_g):
    nbp = x_prompt.shape[0]
    mod = _adaln(jnp.concatenate([c_prompt, c_sample], axis=0), adaln_w, adaln_b)
    wts = _prep_weights(mlstm_w_in, mlstm_b_gate, mlstm_w_out, attn_w_in, attn_w_out, ffn_w_up, ffn_w_down)
    bias_tbl = _bias_table(rel_bias)
    run = functools.partial(_trunk, wts=wts, bias_tbl=bias_tbl, norm_g=norm_g,
                            mlstm_head_g=mlstm_head_g, attn_sink=attn_sink, ffn_conv_w=ffn_conv_w,
                            ffn_conv_b=ffn_conv_b, final_g=final_g, tm=1024, tm_ffn=1024)
    return run(x_prompt, mod[:, :nbp]), run(x_sample, mod[:, nbp:])
```

```python
import functools
import math

import jax
import jax.numpy as jnp
from jax import lax
from jax.experimental import pallas as pl
from jax.experimental.pallas import tpu as pltpu

F32 = jnp.float32
BF16 = jnp.bfloat16

D_MODEL = 1024
DEPTH = 2
M_HEADS = 4
M_DK = 256
M_DV = 256
A_HEADS = 16
A_KV = 4
A_GROUP = A_HEADS // A_KV
A_HD = 64
WINDOW = 128
A_BLOCK = 128
A_SPAN = A_BLOCK + 2 * WINDOW
NUM_BUCKETS = 32
MAX_DISTANCE = 128
D_FF = 2816
EPS = 1e-6
LOG2E = 1.4426950408889634

LANES = 128
MXU_DIM = 256
VMEM_LIMIT = 56 * 1024 * 1024

SCAN_CHUNK = 256
N_GATES = 4 * M_HEADS
HALO = 8
FFN_CHUNK = MXU_DIM
SWA_SUB = 8


def _cparams(n_axes):
    return pltpu.CompilerParams(dimension_semantics=("parallel",) * n_axes,
                                vmem_limit_bytes=VMEM_LIMIT)


def _resident(shape, layer=None):
    nd = len(shape)
    if layer is None:
        return pl.BlockSpec(shape, lambda *_: (0,) * nd, pipeline_mode=pl.Buffered(1))
    return pl.BlockSpec((1,) + tuple(shape), lambda *_: (layer,) + (0,) * nd,
                        pipeline_mode=pl.Buffered(1))


def _norm_mod(x, g, sc, sh):
    ms = jnp.mean(x * x, axis=-1, keepdims=True)
    return x * lax.rsqrt(ms + EPS) * (g * (1.0 + sc)) + sh


def _adaln_kernel(c_ref, w_ref, b_ref, o_ref):
    c = c_ref[...]
    s = c * jax.nn.sigmoid(c)
    o_ref[0] = jnp.dot(s.astype(BF16), w_ref[0].astype(BF16), preferred_element_type=F32) + b_ref[0]


def _adaln(c_all, adaln_w, adaln_b):
    nb = c_all.shape[0]
    n = adaln_w.shape[-1]
    tn = 1536
    return pl.pallas_call(
        _adaln_kernel,
        grid=(DEPTH, n // tn),
        in_specs=[pl.BlockSpec((nb, D_MODEL), lambda l, j: (0, 0)),
                  pl.BlockSpec((1, D_MODEL, tn), lambda l, j: (l, 0, j)),
                  pl.BlockSpec((1, 1, tn), lambda l, j: (l, 0, j))],
        out_specs=pl.BlockSpec((1, nb, tn), lambda l, j: (l, 0, j)),
        out_shape=jax.ShapeDtypeStruct((DEPTH, nb, n), F32),
        compiler_params=_cparams(2),
        name="adaln",
    )(c_all, adaln_w, adaln_b.reshape(DEPTH, 1, n))


def _inproj_split_kernel(x_ref, g_ref, sc_ref, sh_ref, wk_ref, wt_ref, k_ref, t_ref, *rest,
                         tn, chunk, n_extra):
    h = _norm_mod(x_ref[0], g_ref[...], sc_ref[0, 0], sh_ref[0, 0]).astype(BF16)
    tm = h.shape[0]
    nk = wk_ref.shape[1]
    tk = min(tn, nk)
    hpt = tk // MXU_DIM
    for j in range(nk // tk):
        res = jnp.dot(h, wk_ref[:, j * tk:(j + 1) * tk], preferred_element_type=F32).astype(BF16)
        for hh in range(hpt):
            k_ref[0, j * hpt + hh] = res[:, hh * MXU_DIM:(hh + 1) * MXU_DIM]
    nt = wt_ref.shape[0] - n_extra
    for j in range(nt // tn):
        hi = (j + 1) * tn + (n_extra if j == nt // tn - 1 else 0)
        res = lax.dot_general(wt_ref[j * tn:hi, :], h, (((1,), (1,)), ((), ())),
                              preferred_element_type=F32)
        for cc in range(tm // chunk):
            cols = slice(cc * chunk, (cc + 1) * chunk)
            t_ref[0, cc, j * tn:(j + 1) * tn, :] = res[:tn, cols].astype(BF16)
            if hi > (j + 1) * tn:
                rest[0][0, cc] = res[tn:, cols]


def _inproj_split(x, g, mod4, w_k, w_t, *, tm, chunk, tn, n_extra=0):
    bsz, s, d = x.shape
    nk = w_k.shape[1]
    nt = w_t.shape[0] - n_extra
    out_specs = [pl.BlockSpec((1, nk // MXU_DIM, tm, MXU_DIM), lambda b, i: (b, 0, i, 0)),
                 pl.BlockSpec((1, tm // chunk, nt, chunk), lambda b, i: (b, i, 0, 0))]
    out_shape = [jax.ShapeDtypeStruct((bsz, nk // MXU_DIM, s, MXU_DIM), BF16),
                 jax.ShapeDtypeStruct((bsz, s // chunk, nt, chunk), BF16)]
    if n_extra:
        out_specs.append(pl.BlockSpec((1, tm // chunk, n_extra, chunk), lambda b, i: (b, i, 0, 0)))
        out_shape.append(jax.ShapeDtypeStruct((bsz, s // chunk, n_extra, chunk), F32))
    return pl.pallas_call(
        functools.partial(_inproj_split_kernel, tn=tn, chunk=chunk, n_extra=n_extra),
        grid=(bsz, s // tm),
        in_specs=[pl.BlockSpec((1, tm, d), lambda b, i: (b, i, 0)),
                  _resident((1, d)),
                  pl.BlockSpec((1, 1, 1, d), lambda b, i: (b, 1, 0, 0)),
                  pl.BlockSpec((1, 1, 1, d), lambda b, i: (b, 0, 0, 0)),
                  _resident((d, nk)), _resident((nt + n_extra, d))],
        out_specs=out_specs, out_shape=out_shape,
        compiler_params=_cparams(2),
        name="inproj_m" if n_extra else "inproj_a",
    )(x, g.reshape(1, d), mod4, mod4, w_k, w_t)


N_PREP = 6


def _gateprep_kernel(g_ref, b_ref, p_ref, *, nc, chunk):
    s = nc * chunk
    gt = jnp.concatenate([g_ref[0, c] + b_ref[...] for c in range(nc)], axis=1)
    ig = gt[0:8]
    fg = gt[8:16]
    row = lax.broadcasted_iota(jnp.int32, (8, s), 0)
    pos = lax.broadcasted_iota(jnp.int32, (8, s), 1) & (chunk - 1)
    fwd = row < M_HEADS

    def seg_scan(x, op, ident):
        k = 1
        while k < chunk:
            sh_f = jnp.where(pos >= k, pltpu.roll(x, k, axis=1), ident)
            sh_b = jnp.where(pos < chunk - k, pltpu.roll(x, s - k, axis=1), ident)
            x = op(x, jnp.where(fwd, sh_f, sh_b))
            k *= 2
        return x

    lf = jnp.minimum(fg, 0.0) - jnp.log1p(jnp.exp(-jnp.abs(fg)))
    bsum = seg_scan(lf, jnp.add, 0.0)
    beta = ig - bsum
    cmax = seg_scan(beta, jnp.maximum, -jnp.inf)

    fwd1 = fwd[:, :1]
    g_max = [jnp.max(beta[:, c * chunk:(c + 1) * chunk], axis=1, keepdims=True) for c in range(nc)]
    b_last = [jnp.where(fwd1, bsum[:, (c + 1) * chunk - 1:(c + 1) * chunk], bsum[:, c * chunk:c * chunk + 1])
              for c in range(nc)]
    m_f = [None] * nc
    m_b = [None] * nc
    mf = jnp.zeros((8, 1), F32)
    mb = jnp.zeros((8, 1), F32)
    for i in range(nc):
        j = nc - 1 - i
        m_f[i] = mf
        m_b[j] = mb
        mf = b_last[i] + jnp.maximum(mf, g_max[i])
        mb = b_last[j] + jnp.maximum(mb, g_max[j])
    wide = lambda v: jnp.broadcast_to(v, (8, chunk))
    m_in = jnp.concatenate([wide(jnp.where(fwd1, m_f[c], m_b[c])) for c in range(nc)], axis=1)
    g_all = jnp.concatenate([wide(g_max[c]) for c in range(nc)], axis=1)

    mx = jnp.maximum(cmax, m_in)
    m_c = jnp.maximum(m_in, g_all)
    quantities = (beta * LOG2E, mx * LOG2E, jnp.exp(m_in - mx), jnp.exp(-(bsum + mx)), jnp.exp(beta - m_c),
                  jnp.exp(m_in - m_c))
    row16 = lax.broadcasted_iota(jnp.int32, (16, s), 0)
    for h in range(M_HEADS):
        tile = jnp.zeros((16, s), F32)
        for d in range(2):
            for qi, val in enumerate(quantities):
                src = val[d * M_HEADS + h:d * M_HEADS + h + 1]
                tile = jnp.where(row16 == d * 8 + qi, jnp.broadcast_to(src, (16, s)), tile)
        for c in range(nc):
            p_ref[0, c, h] = tile[:, c * chunk:(c + 1) * chunk]


def _gateprep(gates, bias, chunk):
    bsz, nc, ng, _ = gates.shape
    return pl.pallas_call(
        functools.partial(_gateprep_kernel, nc=nc, chunk=chunk),
        grid=(bsz,),
        in_specs=[pl.BlockSpec((1, nc, ng, chunk), lambda b: (b, 0, 0, 0)),
                  _resident((ng, chunk))],
        out_specs=pl.BlockSpec((1, nc, M_HEADS, 16, chunk), lambda b: (b, 0, 0, 0, 0)),
        out_shape=jax.ShapeDtypeStruct((bsz, nc, M_HEADS, 16, chunk), F32),
        compiler_params=_cparams(1),
        name="gateprep",
    )(gates, bias)


def _mlstm_kernel(qt_ref, vt_ref, ot_ref, k_ref, p_ref, hg_ref, at_ref, acc, ct_scr, n_scr,
                  *, nc, chunk):
    L = chunk
    ct_scr[...] = jnp.zeros_like(ct_scr)
    n_scr[...] = jnp.zeros_like(n_scr)
    s_idx = lax.broadcasted_iota(jnp.int32, (L, L), 0)
    t_idx = lax.broadcasted_iota(jnp.int32, (L, L), 1)
    tri = (s_idx <= t_idx, s_idx >= t_idx)
    n_pad = 16

    def body(i, carry, *, finish):
        cs = (i, nc - 1 - i)
        qt = [qt_ref[0, c] for c in cs]
        vt = [vt_ref[0, c] for c in cs]
        k = [k_ref[0, 0, pl.ds(pl.multiple_of(c * L, L), L), :] for c in cs]
        prm = [p_ref[0, cs[d], 0, d * 8:(d + 1) * 8, :] for d in range(2)]
        ct_old = [ct_scr[d] for d in range(2)]
        n_old = [n_scr[d] for d in range(2)]
        r1 = []
        for d in range(2):
            lhs = jnp.concatenate([k[d], ct_old[d].astype(BF16),
                                   jnp.concatenate([n_old[d], n_old[d]], axis=0).astype(BF16)], axis=0)
            r1.append(jnp.dot(lhs, qt[d], preferred_element_type=F32))
        r2 = []
        for d in range(2):
            w = prm[d][4:5]
            vts = (vt[d].astype(F32) * w).astype(BF16)
            lhs2 = jnp.concatenate([vts, jnp.broadcast_to(w, (n_pad, L)).astype(BF16)], axis=0)
            r2.append(jnp.dot(lhs2, k[d], preferred_element_type=F32))
        for d in range(2):
            beta, mx, w_inter, neg_mt, _, decay = (prm[d][j:j + 1] for j in range(N_PREP))
            d_t = jnp.where(tri[d], jnp.exp2(jnp.broadcast_to(beta, (L, L)).T - mx), 0.0)
            p_t = r1[d][0:L] * d_t
            inter = r1[d][L:L + M_DV]
            qn = r1[d][L + M_DV:L + M_DV + 1]
            num = jnp.dot(vt[d], p_t.astype(BF16), preferred_element_type=F32) + w_inter * inter
            den = jnp.sum(p_t, axis=0, keepdims=True) + w_inter * qn
            hout = num * (1.0 / jnp.maximum(jnp.abs(den), neg_mt))
            if finish:
                hs = acc[cs[d]] + hout
                ms = jnp.mean(hs * hs, axis=0, keepdims=True)
                y = hs * lax.rsqrt(ms + EPS) * hg_ref[0]
                gate = jax.nn.sigmoid(ot_ref[0, cs[d]].astype(F32))
                at_ref[0, cs[d]] = (y * gate).astype(BF16)
            else:
                acc[cs[d]] = hout
            ct_scr[d] = decay * ct_old[d] + r2[d][0:M_DV]
            n_scr[d] = decay * n_old[d] + r2[d][M_DV:M_DV + 8]
        return carry

    half = nc // 2
    lax.fori_loop(0, half, functools.partial(body, finish=False), 0, unroll=half)
    lax.fori_loop(half, nc, functools.partial(body, finish=True), 0, unroll=2)


def _mlstm_scan(k_tok, qvo_t, prep, head_g, chunk):
    bsz, _, s, _ = k_tok.shape
    nc = s // chunk
    assert nc % 8 == 0
    blk_t = lambda off: pl.BlockSpec((1, nc, M_DK, chunk), lambda b, h: (b, 0, off + h, 0))
    return pl.pallas_call(
        functools.partial(_mlstm_kernel, nc=nc, chunk=chunk),
        grid=(bsz, M_HEADS),
        in_specs=[blk_t(0), blk_t(M_HEADS), blk_t(2 * M_HEADS),
                  pl.BlockSpec((1, 1, s, M_DK), lambda b, h: (b, h, 0, 0)),
                  pl.BlockSpec((1, nc, 1, 16, chunk), lambda b, h: (b, 0, h, 0, 0)),
                  pl.BlockSpec((1, M_DV, 1), lambda b, h: (h, 0, 0))],
        out_specs=pl.BlockSpec((1, nc, M_DV, chunk), lambda b, h: (b, 0, h, 0)),
        out_shape=jax.ShapeDtypeStruct((bsz, nc, M_HEADS * M_DV, chunk), BF16),
        scratch_shapes=[pltpu.VMEM((nc, M_DV, chunk), F32),
                        pltpu.VMEM((2, M_DV, M_DK), F32),
                        pltpu.VMEM((2, 8, M_DK), F32)],
        compiler_params=_cparams(2),
        name="mlstm_scan",
    )(qvo_t, qvo_t, qvo_t, k_tok, prep, head_g.reshape(M_HEADS, M_DV, 1))


def _mix_ffn_kernel(a_ref, ap_ref, an_ref, x_ref, xp_ref, xn_ref, wo_ref, g1_ref, g_ref, sc_ref, sh_ref,
                    gate_ref, wup_ref, cw_ref, cb_ref, wdn_ref, fg_ref, o_ref, h_scr, a_scr,
                    *, tm, final):
    i = pl.program_id(1)
    last = pl.num_programs(1) - 1
    wo = wo_ref[...]
    g1 = g1_ref[0, 0]

    def mix(a_t):
        return lax.dot_general(a_t, wo, (((0,), (0,)), ((), ())), preferred_element_type=F32)

    a_main = jnp.concatenate([a_ref[0, cc] for cc in range(a_ref.shape[1])], axis=1)
    x1 = x_ref[0] + g1 * mix(a_main)
    x1p = xp_ref[0] + g1 * mix(ap_ref[0, 0][:, LANES - HALO:])
    x1n = xn_ref[0] + g1 * mix(an_ref[0, 0][:, :HALO])

    g = g_ref[...]
    sc = sc_ref[0, 0]
    sh = sh_ref[0, 0]
    hp = jnp.where(i > 0, _norm_mod(x1p, g, sc, sh), 0.0)
    hn = jnp.where(i < last, _norm_mod(x1n, g, sc, sh), 0.0)
    h_scr[0:2 * HALO] = jnp.concatenate([hn, hp], axis=0).astype(BF16)
    h_scr[2 * HALO:] = _norm_mod(x1, g, sc, sh).astype(BF16)
    h = h_scr[...]
    ext = tm + 2 * HALO

    def conv(u, off):
        w0 = cw_ref[0, 0:1, off:off + FFN_CHUNK]
        w1 = cw_ref[0, 1:2, off:off + FFN_CHUNK]
        w2 = cw_ref[0, 2:3, off:off + FFN_CHUNK]
        prev = pltpu.roll(u, 1, axis=0)
        nxt = pltpu.roll(u, ext - 1, axis=0)
        out = w0 * prev + w1 * u + w2 * nxt + cb_ref[0, :, off:off + FFN_CHUNK]
        return out[2 * HALO:]

    for f in range(D_FF // FFN_CHUNK):
        og = f * FFN_CHUNK
        ov = D_FF + f * FFN_CHUNK
        ug = conv(jnp.dot(h, wup_ref[0, :, og:og + FFN_CHUNK], preferred_element_type=F32), og)
        uv = conv(jnp.dot(h, wup_ref[0, :, ov:ov + FFN_CHUNK], preferred_element_type=F32), ov)
        a_scr[:, og:og + FFN_CHUNK] = (ug * jax.nn.sigmoid(ug) * uv).astype(BF16)

    y = jnp.dot(a_scr[...], wdn_ref[0], preferred_element_type=F32)
    out = x1 + gate_ref[0, 0] * y
    if final:
        ms = jnp.mean(out * out, axis=-1, keepdims=True)
        out = out * lax.rsqrt(ms + EPS) * fg_ref[...]
    o_ref[0] = out


def _mix_ffn(a_t, w_o, x, g, mod4, layer, w_up, conv_w, conv_b, w_down, final_g, *, tm, chunk, final):
    bsz, s, d = x.shape
    kdim = w_o.shape[0]
    nc = s // chunk
    cpt = tm // chunk
    hb = tm // HALO
    nh = s // HALO
    return pl.pallas_call(
        functools.partial(_mix_ffn_kernel, tm=tm, final=final),
        grid=(bsz, s // tm),
        in_specs=[pl.BlockSpec((1, cpt, kdim, chunk), lambda b, i: (b, i, 0, 0)),
                  pl.BlockSpec((1, 1, kdim, LANES),
                               lambda b, i: (b, jnp.maximum(i * cpt - 1, 0), 0, chunk // LANES - 1)),
                  pl.BlockSpec((1, 1, kdim, LANES),
                               lambda b, i: (b, jnp.minimum((i + 1) * cpt, nc - 1), 0, 0)),
                  pl.BlockSpec((1, tm, d), lambda b, i: (b, i, 0)),
                  pl.BlockSpec((1, HALO, d), lambda b, i: (b, jnp.maximum(i * hb - 1, 0), 0)),
                  pl.BlockSpec((1, HALO, d), lambda b, i: (b, jnp.minimum((i + 1) * hb, nh - 1), 0)),
                  _resident((kdim, d)),
                  pl.BlockSpec((1, 1, 1, d), lambda b, i: (b, 2, 0, 0)),
                  _resident((1, d)),
                  pl.BlockSpec((1, 1, 1, d), lambda b, i: (b, 4, 0, 0)),
                  pl.BlockSpec((1, 1, 1, d), lambda b, i: (b, 3, 0, 0)),
                  pl.BlockSpec((1, 1, 1, d), lambda b, i: (b, 5, 0, 0)),
                  _resident((d, 2 * D_FF), layer),
                  _resident((3, 2 * D_FF), layer),
                  _resident((1, 2 * D_FF), layer),
                  _resident((D_FF, d), layer),
                  _resident((1, d))],
        out_specs=pl.BlockSpec((1, tm, d), lambda b, i: (b, i, 0)),
        out_shape=jax.ShapeDtypeStruct((bsz, s, d), F32),
        scratch_shapes=[pltpu.VMEM((tm + 2 * HALO, d), BF16),
                        pltpu.VMEM((tm, D_FF), BF16)],
        compiler_params=_cparams(2),
        name="mix_ffn_final" if final else "mix_ffn",
    )(a_t, a_t, a_t, x, x, x, w_o, mod4, g.reshape(1, d), mod4, mod4, mod4, w_up, conv_w,
      conv_b.reshape(DEPTH, 1, -1), w_down, final_g.reshape(1, d))


def _t5_bucket(rel):
    nb = NUM_BUCKETS // 2
    max_exact = nb // 2
    ret = jnp.where(rel > 0, nb, 0)
    n = jnp.abs(rel)
    nf = jnp.maximum(n, 1).astype(jnp.float32)
    large = max_exact + (jnp.log(nf / max_exact) / math.log(MAX_DISTANCE / max_exact)
                         * (nb - max_exact)).astype(jnp.int32)
    large = jnp.minimum(large, nb - 1)
    return ret + jnp.where(n < max_exact, n, large)


N_EDGE = 3


def _bias_kernel(bucket_ref, rb_ref, o_ref):
    hh = pl.program_id(0)
    head = (hh % A_KV) * A_GROUP + hh // A_KV
    bucket = bucket_ref[...]
    c = lax.broadcasted_iota(jnp.int32, bucket.shape, 0)
    q = lax.broadcasted_iota(jnp.int32, bucket.shape, 1)
    band = jnp.abs(c - WINDOW - q) <= WINDOW
    val = jnp.zeros(bucket.shape, F32)
    for kb in range(NUM_BUCKETS):
        val = jnp.where(bucket == kb, rb_ref[kb, head], val)
    val = val * LOG2E
    o_ref[0, 0] = jnp.where(band & (c >= WINDOW), val, -jnp.inf)
    o_ref[1, 0] = jnp.where(band, val, -jnp.inf)
    o_ref[N_EDGE - 1, 0] = jnp.where(band & (c < WINDOW + A_BLOCK), val, -jnp.inf)


def _bias_table(rel_bias):
    c_idx = jnp.arange(A_SPAN)[:, None]
    q_idx = jnp.arange(A_BLOCK)[None, :]
    bucket = _t5_bucket(c_idx - WINDOW - q_idx).astype(jnp.int32)
    return pl.pallas_call(
        _bias_kernel,
        grid=(A_HEADS,),
        in_specs=[pl.BlockSpec((A_SPAN, A_BLOCK), lambda h: (0, 0)),
                  pl.BlockSpec(memory_space=pltpu.SMEM)],
        out_specs=pl.BlockSpec((N_EDGE, 1, A_SPAN, A_BLOCK), lambda h: (0, h, 0, 0)),
        out_shape=jax.ShapeDtypeStruct((N_EDGE, A_HEADS, A_SPAN, A_BLOCK), F32),
        compiler_params=_cparams(1),
        name="bias_table",
    )(bucket, rel_bias)


def _swa_kernel(qt_ref, kp_ref, kc_ref, kn_ref, vp_ref, vc_ref, vn_ref, bias_ref, sink_ref, o_ref):
    j = pl.program_id(1)
    last = pl.num_programs(1) - 1
    n_sub = SWA_SUB
    kblk = ([kp_ref[0, 0]] + [kc_ref[0, 0, u * A_BLOCK:(u + 1) * A_BLOCK] for u in range(n_sub)]
            + [kn_ref[0, 0]])
    vblk = [vp_ref[0, 0]] + [vc_ref[0, u] for u in range(n_sub)] + [vn_ref[0, 0]]
    var = [1] * n_sub
    var[0] = jnp.where(j == 0, 0, 1)
    var[-1] = jnp.where(j == last, N_EDGE - 1, 1)
    lane_kv = lax.broadcasted_iota(jnp.int32, (A_SPAN, A_KV * A_HD), 1) // A_HD
    row_kv = lax.broadcasted_iota(jnp.int32, (A_KV * A_HD, A_SPAN), 0) // A_HD
    grp = lax.broadcasted_iota(jnp.int32, (1, A_GROUP * A_BLOCK), 1) // A_BLOCK
    groups = range(A_GROUP)
    kcat = []
    vstack = []
    qp = []
    for u in range(n_sub):
        kcat.append(jnp.concatenate(kblk[u:u + 3], axis=0))
        vcat = jnp.concatenate(vblk[u:u + 3], axis=1)
        vstack.append(jnp.concatenate([jnp.where(row_kv == kv, vcat, jnp.zeros_like(vcat))
                                       for kv in range(A_KV)], axis=1))
        qp.append(jnp.concatenate([qt_ref[0, u, g * MXU_DIM:(g + 1) * MXU_DIM, :] for g in groups],
                                  axis=1))
    ps = [[] for _ in range(n_sub)]
    rden = [[] for _ in range(n_sub)]
    for kv in range(A_KV):
        sink = jnp.zeros((1, A_GROUP * A_BLOCK), F32)
        for g in groups:
            sink = jnp.where(grp == g, sink_ref[kv * A_GROUP + g], sink)
        sink = sink * LOG2E
        for u in range(n_sub):
            bias = jnp.concatenate([bias_ref[var[u], g * A_KV + kv] for g in groups], axis=1)
            k_kv = jnp.where(lane_kv == kv, kcat[u], jnp.zeros_like(kcat[u]))
            s = jnp.dot(k_kv, qp[u], preferred_element_type=F32) + bias
            m = jnp.maximum(jnp.max(s, axis=0, keepdims=True), sink)
            p = jnp.exp2(s - m)
            rden[u].append(1.0 / (jnp.sum(p, axis=0, keepdims=True) + jnp.exp2(sink - m)))
            ps[u].append(p.astype(BF16))
    for u in range(n_sub):
        out = jnp.dot(vstack[u], jnp.concatenate(ps[u], axis=0), preferred_element_type=F32)
        out = jnp.concatenate([out[kv * A_HD:(kv + 1) * A_HD] * rden[u][kv] for kv in range(A_KV)], axis=0)
        for g in groups:
            o_ref[0, u, g * MXU_DIM:(g + 1) * MXU_DIM, :] = (
                out[:, g * A_BLOCK:(g + 1) * A_BLOCK].astype(BF16))


def _swa(k_tok, qv_t, bias_tbl, sink):
    bsz, _, s, _ = k_tok.shape
    nb = s // A_BLOCK
    ns = SWA_SUB
    assert nb % ns == 0 and nb >= 2 * ns
    qw = A_HEADS * A_HD
    vrow = qw // MXU_DIM
    prev = lambda j: jnp.maximum(ns * j - 1, 0)
    nxt = lambda j: jnp.minimum(ns * j + ns, nb - 1)
    k_edge = lambda row: pl.BlockSpec((1, 1, A_BLOCK, MXU_DIM), lambda b, j: (b, 0, row(j), 0))
    v_edge = lambda row: pl.BlockSpec((1, 1, MXU_DIM, A_BLOCK), lambda b, j: (b, row(j), vrow, 0))
    return pl.pallas_call(
        _swa_kernel,
        grid=(bsz, nb // ns),
        in_specs=[pl.BlockSpec((1, ns, qw, A_BLOCK), lambda b, j: (b, j, 0, 0)),
                  k_edge(prev), pl.BlockSpec((1, 1, ns * A_BLOCK, MXU_DIM), lambda b, j: (b, 0, j, 0)),
                  k_edge(nxt),
                  v_edge(prev), pl.BlockSpec((1, ns, MXU_DIM, A_BLOCK), lambda b, j: (b, j, vrow, 0)),
                  v_edge(nxt),
                  _resident((N_EDGE, A_HEADS, A_SPAN, A_BLOCK)),
                  pl.BlockSpec(memory_space=pltpu.SMEM)],
        out_specs=pl.BlockSpec((1, ns, qw, A_BLOCK), lambda b, j: (b, j, 0, 0)),
        out_shape=jax.ShapeDtypeStruct((bsz, nb, qw, A_BLOCK), BF16),
        compiler_params=_cparams(2),
        name="swa",
    )(qv_t, k_tok, k_tok, k_tok, qv_t, qv_t, qv_t, bias_tbl, sink)


def _prep_weights(mlstm_w_in, mlstm_b_gate, mlstm_w_out, attn_w_in, attn_w_out, ffn_w_up, ffn_w_down):
    qk = M_HEADS * M_DK
    w_m = mlstm_w_in[0]
    w_q = w_m[:, :qk] * (M_DK ** -0.5)
    perm = lambda g: g.reshape(-1, 2, 2, M_HEADS).swapaxes(1, 2).reshape(-1, N_GATES)
    w_t = jnp.concatenate([w_q, w_m[:, 2 * qk:4 * qk], perm(w_m[:, 4 * qk:])], axis=1).T.astype(BF16)
    w_k = w_m[:, qk:2 * qk].astype(BF16)
    b_gate = jnp.broadcast_to(perm(mlstm_b_gate[0][None]).T, (N_GATES, SCAN_CHUNK))
    qw = A_HEADS * A_HD
    kw = A_KV * A_HD
    w_a = attn_w_in[0]
    wq = w_a[:, :qw].reshape(D_MODEL, A_KV, A_GROUP, A_HD).transpose(0, 2, 1, 3).reshape(D_MODEL, qw)
    w_attn_k = w_a[:, qw:qw + kw].astype(BF16)
    w_attn_t = jnp.concatenate([wq * (A_HD ** -0.5 * LOG2E), w_a[:, qw + kw:]], axis=1).T.astype(BF16)
    wo_attn = attn_w_out[0].reshape(A_KV, A_GROUP, A_HD, D_MODEL).transpose(1, 0, 2, 3)
    wo_attn = wo_attn.reshape(qw, D_MODEL).astype(BF16)
    return dict(w_k=w_k, w_t=w_t, b_gate=b_gate,
                wo_m=mlstm_w_out[0].astype(BF16), w_attn_k=w_attn_k, w_attn_t=w_attn_t, wo_attn=wo_attn,
                w_up=ffn_w_up.astype(BF16), w_down=ffn_w_down.astype(BF16))


def _trunk(x, mod, wts, bias_tbl, norm_g, mlstm_head_g, attn_sink, ffn_conv_w, ffn_conv_b, final_g,
           *, tm, tm_ffn):
    bsz, s, d = x.shape
    mod4 = mod[0].reshape(bsz, 6, 1, d)
    k_tok, qvo_t, gates = _inproj_split(x, norm_g[0, 0], mod4, wts["w_k"], wts["w_t"],
                                        tm=tm, chunk=SCAN_CHUNK, tn=512, n_extra=N_GATES)
    prep = _gateprep(gates, wts["b_gate"], SCAN_CHUNK)
    a_t = _mlstm_scan(k_tok, qvo_t, prep, mlstm_head_g[0], SCAN_CHUNK)
    x = _mix_ffn(a_t, wts["wo_m"], x, norm_g[0, 1], mod4, 0, wts["w_up"], ffn_conv_w, ffn_conv_b,
                 wts["w_down"], final_g, tm=tm_ffn, chunk=SCAN_CHUNK, final=False)
    mod4 = mod[1].reshape(bsz, 6, 1, d)
    k_tok, qv_t = _inproj_split(x, norm_g[1, 0], mod4, wts["w_attn_k"], wts["w_attn_t"],
                                tm=tm, chunk=A_BLOCK, tn=640)
    a_t = _swa(k_tok, qv_t, bias_tbl, attn_sink[0])
    return _mix_ffn(a_t, wts["wo_attn"], x, norm_g[1, 1], mod4, 1, wts["w_up"], ffn_conv_w,
                    ffn_conv_b, wts["w_down"], final_g, tm=tm_ffn, chunk=A_BLOCK, final=True)


def kernel(x_prompt, x_sample, c_prompt, c_sample, adaln_w, adaln_b, norm_g, mlstm_w_in, mlstm_b_gate, mlstm_head_g, mlstm_w_out, attn_w_in, attn_sink, attn_w_out, rel_bias, ffn_w_up, ffn_conv_w, ffn_conv_b, ffn_w_down, final_g):
    nbp = x_prompt.shape[0]
    mod = _adaln(jnp.concatenate([c_prompt, c_sample], axis=0), adaln_w, adaln_b)
    wts = _prep_weights(mlstm_w_in, mlstm_b_gate, mlstm_w_out, attn_w_in, attn_w_out, ffn_w_up, ffn_w_down)
    bias_tbl = _bias_table(rel_bias)
    run = functools.partial(_trunk, wts=wts, bias_tbl=bias_tbl, norm_g=norm_g,
                            mlstm_head_g=mlstm_head_g, attn_sink=attn_sink, ffn_conv_w=ffn_conv_w,
                            ffn_conv_b=ffn_conv_b, final_g=final_g, tm=1024, tm_ffn=1024)
    return run(x_prompt, mod[:, :nbp]), run(x_sample, mod[:, nbp:])
```

```python
import functools
import math

import jax
import jax.numpy as jnp
from jax import lax
from jax.experimental import pallas as pl
from jax.experimental.pallas import tpu as pltpu

F32 = jnp.float32
BF16 = jnp.bfloat16

D_MODEL = 1024
DEPTH = 2
M_HEADS = 4
M_DK = 256
M_DV = 256
A_HEADS = 16
A_KV = 4
A_GROUP = A_HEADS // A_KV
A_HD = 64
WINDOW = 128
A_BLOCK = 128
A_SPAN = A_BLOCK + 2 * WINDOW
NUM_BUCKETS = 32
MAX_DISTANCE = 128
D_FF = 2816
EPS = 1e-6
LOG2E = 1.4426950408889634

LANES = 128
MXU_DIM = 256
VMEM_LIMIT = 56 * 1024 * 1024

SCAN_CHUNK = 256
N_GATES = 4 * M_HEADS
HALO = 8
FFN_CHUNK = MXU_DIM
SWA_SUB = 8


def _cparams(n_axes):
    return pltpu.CompilerParams(dimension_semantics=("parallel",) * n_axes,
                                vmem_limit_bytes=VMEM_LIMIT)


def _resident(shape, layer=None):
    nd = len(shape)
    if layer is None:
        return pl.BlockSpec(shape, lambda *_: (0,) * nd, pipeline_mode=pl.Buffered(1))
    return pl.BlockSpec((1,) + tuple(shape), lambda *_: (layer,) + (0,) * nd,
                        pipeline_mode=pl.Buffered(1))


def _norm_mod(x, g, sc, sh):
    ms = jnp.mean(x * x, axis=-1, keepdims=True)
    return x * lax.rsqrt(ms + EPS) * (g * (1.0 + sc)) + sh


def _adaln_kernel(c_ref, w_ref, b_ref, o_ref):
    c = c_ref[...]
    s = c * jax.nn.sigmoid(c)
    o_ref[0] = jnp.dot(s.astype(BF16), w_ref[0].astype(BF16), preferred_element_type=F32) + b_ref[0]


def _adaln(c_all, adaln_w, adaln_b):
    nb = c_all.shape[0]
    n = adaln_w.shape[-1]
    tn = 1536
    return pl.pallas_call(
        _adaln_kernel,
        grid=(DEPTH, n // tn),
        in_specs=[pl.BlockSpec((nb, D_MODEL), lambda l, j: (0, 0)),
                  pl.BlockSpec((1, D_MODEL, tn), lambda l, j: (l, 0, j)),
                  pl.BlockSpec((1, 1, tn), lambda l, j: (l, 0, j))],
        out_specs=pl.BlockSpec((1, nb, tn), lambda l, j: (l, 0, j)),
        out_shape=jax.ShapeDtypeStruct((DEPTH, nb, n), F32),
        compiler_params=_cparams(2),
        name="adaln",
    )(c_all, adaln_w, adaln_b.reshape(DEPTH, 1, n))


def _inproj_split_kernel(x_ref, g_ref, sc_ref, sh_ref, wk_ref, wt_ref, k_ref, t_ref, *rest,
                         tn, chunk, n_extra):
    h = _norm_mod(x_ref[0], g_ref[...], sc_ref[0, 0], sh_ref[0, 0]).astype(BF16)
    tm = h.shape[0]
    nk = wk_ref.shape[1]
    tk = min(tn, nk)
    hpt = tk // MXU_DIM
    for j in range(nk // tk):
        res = jnp.dot(h, wk_ref[:, j * tk:(j + 1) * tk], preferred_element_type=F32).astype(BF16)
        for hh in range(hpt):
            k_ref[0, j * hpt + hh] = res[:, hh * MXU_DIM:(hh + 1) * MXU_DIM]
    nt = wt_ref.shape[0] - n_extra
    for j in range(nt // tn):
        hi = (j + 1) * tn + (n_extra if j == nt // tn - 1 else 0)
        res = lax.dot_general(wt_ref[j * tn:hi, :], h, (((1,), (1,)), ((), ())),
                              preferred_element_type=F32)
        for cc in range(tm // chunk):
            cols = slice(cc * chunk, (cc + 1) * chunk)
            t_ref[0, cc, j * tn:(j + 1) * tn, :] = res[:tn, cols].astype(BF16)
            if hi > (j + 1) * tn:
                rest[0][0, cc] = res[tn:, cols]


def _inproj_split(x, g, mod4, w_k, w_t, *, tm, chunk, tn, n_extra=0):
    bsz, s, d = x.shape
    nk = w_k.shape[1]
    nt = w_t.shape[0] - n_extra
    out_specs = [pl.BlockSpec((1, nk // MXU_DIM, tm, MXU_DIM), lambda b, i: (b, 0, i, 0)),
                 pl.BlockSpec((1, tm // chunk, nt, chunk), lambda b, i: (b, i, 0, 0))]
    out_shape = [jax.ShapeDtypeStruct((bsz, nk // MXU_DIM, s, MXU_DIM), BF16),
                 jax.ShapeDtypeStruct((bsz, s // chunk, nt, chunk), BF16)]
    if n_extra:
        out_specs.append(pl.BlockSpec((1, tm // chunk, n_extra, chunk), lambda b, i: (b, i, 0, 0)))
        out_shape.append(jax.ShapeDtypeStruct((bsz, s // chunk, n_extra, chunk), F32))
    return pl.pallas_call(
        functools.partial(_inproj_split_kernel, tn=tn, chunk=chunk, n_extra=n_extra),
        grid=(bsz, s // tm),
        in_specs=[pl.BlockSpec((1, tm, d), lambda b, i: (b, i, 0)),
                  _resident((1, d)),
                  pl.BlockSpec((1, 1, 1, d), lambda b, i: (b, 1, 0, 0)),
                  pl.BlockSpec((1, 1, 1, d), lambda b, i: (b, 0, 0, 0)),
                  _resident((d, nk)), _resident((nt + n_extra, d))],
        out_specs=out_specs, out_shape=out_shape,
        compiler_params=_cparams(2),
        name="inproj_m" if n_extra else "inproj_a",
    )(x, g.reshape(1, d), mod4, mod4, w_k, w_t)


N_PREP = 6


GATE_BATCH = 2


def _gateprep_kernel(g_ref, b_ref, p_ref, *, nc, chunk):
    for bb in range(g_ref.shape[0]):
        _gateprep_one(g_ref, b_ref, p_ref, bb, nc=nc, chunk=chunk)


def _gateprep_one(g_ref, b_ref, p_ref, bb, *, nc, chunk):
    s = nc * chunk
    gt = jnp.concatenate([g_ref[bb, c] + b_ref[...] for c in range(nc)], axis=1)
    ig = gt[0:8]
    fg = gt[8:16]
    row = lax.broadcasted_iota(jnp.int32, (8, s), 0)
    pos = lax.broadcasted_iota(jnp.int32, (8, s), 1) & (chunk - 1)
    fwd = row < M_HEADS

    def seg_scan(x, op, ident):
        k = 1
        while k < chunk:
            sh_f = jnp.where(pos >= k, pltpu.roll(x, k, axis=1), ident)
            sh_b = jnp.where(pos < chunk - k, pltpu.roll(x, s - k, axis=1), ident)
            x = op(x, jnp.where(fwd, sh_f, sh_b))
            k *= 2
        return x

    lf = jnp.minimum(fg, 0.0) - jnp.log1p(jnp.exp(-jnp.abs(fg)))
    bsum = seg_scan(lf, jnp.add, 0.0)
    beta = ig - bsum
    cmax = seg_scan(beta, jnp.maximum, -jnp.inf)

    fwd1 = fwd[:, :1]
    g_max = [jnp.max(beta[:, c * chunk:(c + 1) * chunk], axis=1, keepdims=True) for c in range(nc)]
    b_last = [jnp.where(fwd1, bsum[:, (c + 1) * chunk - 1:(c + 1) * chunk], bsum[:, c * chunk:c * chunk + 1])
              for c in range(nc)]
    m_f = [None] * nc
    m_b = [None] * nc
    mf = jnp.zeros((8, 1), F32)
    mb = jnp.zeros((8, 1), F32)
    for i in range(nc):
        j = nc - 1 - i
        m_f[i] = mf
        m_b[j] = mb
        mf = b_last[i] + jnp.maximum(mf, g_max[i])
        mb = b_last[j] + jnp.maximum(mb, g_max[j])
    wide = lambda v: jnp.broadcast_to(v, (8, chunk))
    m_in = jnp.concatenate([wide(jnp.where(fwd1, m_f[c], m_b[c])) for c in range(nc)], axis=1)
    g_all = jnp.concatenate([wide(g_max[c]) for c in range(nc)], axis=1)

    mx = jnp.maximum(cmax, m_in)
    m_c = jnp.maximum(m_in, g_all)
    quantities = (beta * LOG2E, mx * LOG2E, jnp.exp(m_in - mx), jnp.exp(-(bsum + mx)), jnp.exp(beta - m_c),
                  jnp.exp(m_in - m_c))
    row16 = lax.broadcasted_iota(jnp.int32, (16, s), 0)
    for h in range(M_HEADS):
        tile = jnp.zeros((16, s), F32)
        for d in range(2):
            for qi, val in enumerate(quantities):
                src = val[d * M_HEADS + h:d * M_HEADS + h + 1]
                tile = jnp.where(row16 == d * 8 + qi, jnp.broadcast_to(src, (16, s)), tile)
        for c in range(nc):
            p_ref[bb, c, h] = tile[:, c * chunk:(c + 1) * chunk]


def _gateprep(gates, bias, chunk):
    bsz, nc, ng, _ = gates.shape
    return pl.pallas_call(
        functools.partial(_gateprep_kernel, nc=nc, chunk=chunk),
        grid=(bsz // GATE_BATCH,),
        in_specs=[pl.BlockSpec((GATE_BATCH, nc, ng, chunk), lambda b: (b, 0, 0, 0)),
                  _resident((ng, chunk))],
        out_specs=pl.BlockSpec((GATE_BATCH, nc, M_HEADS, 16, chunk), lambda b: (b, 0, 0, 0, 0)),
        out_shape=jax.ShapeDtypeStruct((bsz, nc, M_HEADS, 16, chunk), F32),
        compiler_params=_cparams(1),
        name="gateprep",
    )(gates, bias)


def _mlstm_kernel(qt_ref, vt_ref, ot_ref, k_ref, p_ref, hg_ref, at_ref, acc, ct_scr, n_scr,
                  *, nc, chunk):
    L = chunk
    ct_scr[...] = jnp.zeros_like(ct_scr)
    n_scr[...] = jnp.zeros_like(n_scr)
    s_idx = lax.broadcasted_iota(jnp.int32, (L, L), 0)
    t_idx = lax.broadcasted_iota(jnp.int32, (L, L), 1)
    tri = (s_idx <= t_idx, s_idx >= t_idx)
    n_pad = 16

    def body(i, carry, *, finish):
        cs = (i, nc - 1 - i)
        qt = [qt_ref[0, c] for c in cs]
        vt = [vt_ref[0, c] for c in cs]
        k = [k_ref[0, 0, pl.ds(pl.multiple_of(c * L, L), L), :] for c in cs]
        prm = [p_ref[0, cs[d], 0, d * 8:(d + 1) * 8, :] for d in range(2)]
        ct_old = [ct_scr[d] for d in range(2)]
        n_old = [n_scr[d] for d in range(2)]
        r1 = []
        for d in range(2):
            lhs = jnp.concatenate([k[d], ct_old[d].astype(BF16),
                                   jnp.concatenate([n_old[d], n_old[d]], axis=0).astype(BF16)], axis=0)
            r1.append(jnp.dot(lhs, qt[d], preferred_element_type=F32))
        r2 = []
        for d in range(2):
            w = prm[d][4:5]
            vts = (vt[d].astype(F32) * w).astype(BF16)
            lhs2 = jnp.concatenate([vts, jnp.broadcast_to(w, (n_pad, L)).astype(BF16)], axis=0)
            r2.append(jnp.dot(lhs2, k[d], preferred_element_type=F32))
        for d in range(2):
            beta, mx, w_inter, neg_mt, _, decay = (prm[d][j:j + 1] for j in range(N_PREP))
            d_t = jnp.where(tri[d], jnp.exp2(jnp.broadcast_to(beta, (L, L)).T - mx), 0.0)
            p_t = r1[d][0:L] * d_t
            inter = r1[d][L:L + M_DV]
            qn = r1[d][L + M_DV:L + M_DV + 1]
            num = jnp.dot(vt[d], p_t.astype(BF16), preferred_element_type=F32) + w_inter * inter
            den = jnp.sum(p_t, axis=0, keepdims=True) + w_inter * qn
            hout = num * (1.0 / jnp.maximum(jnp.abs(den), neg_mt))
            if finish:
                hs = acc[cs[d]] + hout
                ms = jnp.mean(hs * hs, axis=0, keepdims=True)
                y = hs * lax.rsqrt(ms + EPS) * hg_ref[0]
                gate = jax.nn.sigmoid(ot_ref[0, cs[d]].astype(F32))
                at_ref[0, cs[d]] = (y * gate).astype(BF16)
            else:
                acc[cs[d]] = hout
            ct_scr[d] = decay * ct_old[d] + r2[d][0:M_DV]
            n_scr[d] = decay * n_old[d] + r2[d][M_DV:M_DV + 8]
        return carry

    half = nc // 2
    lax.fori_loop(0, half, functools.partial(body, finish=False), 0, unroll=half)
    lax.fori_loop(half, nc, functools.partial(body, finish=True), 0, unroll=2)


def _mlstm_scan(k_tok, qvo_t, prep, head_g, chunk):
    bsz, _, s, _ = k_tok.shape
    nc = s // chunk
    assert nc % 8 == 0
    blk_t = lambda off: pl.BlockSpec((1, nc, M_DK, chunk), lambda b, h: (b, 0, off + h, 0))
    return pl.pallas_call(
        functools.partial(_mlstm_kernel, nc=nc, chunk=chunk),
        grid=(bsz, M_HEADS),
        in_specs=[blk_t(0), blk_t(M_HEADS), blk_t(2 * M_HEADS),
                  pl.BlockSpec((1, 1, s, M_DK), lambda b, h: (b, h, 0, 0)),
                  pl.BlockSpec((1, nc, 1, 16, chunk), lambda b, h: (b, 0, h, 0, 0)),
                  pl.BlockSpec((1, M_DV, 1), lambda b, h: (h, 0, 0))],
        out_specs=pl.BlockSpec((1, nc, M_DV, chunk), lambda b, h: (b, 0, h, 0)),
        out_shape=jax.ShapeDtypeStruct((bsz, nc, M_HEADS * M_DV, chunk), BF16),
        scratch_shapes=[pltpu.VMEM((nc, M_DV, chunk), F32),
                        pltpu.VMEM((2, M_DV, M_DK), F32),
                        pltpu.VMEM((2, 8, M_DK), F32)],
        compiler_params=_cparams(2),
        name="mlstm_scan",
    )(qvo_t, qvo_t, qvo_t, k_tok, prep, head_g.reshape(M_HEADS, M_DV, 1))


def _mix_ffn_kernel(a_ref, ap_ref, an_ref, x_ref, xp_ref, xn_ref, wo_ref, g1_ref, g_ref, sc_ref, sh_ref,
                    gate_ref, wup_ref, cw_ref, cb_ref, wdn_ref, fg_ref, o_ref, h_scr, a_scr,
                    *, tm, final):
    i = pl.program_id(1)
    last = pl.num_programs(1) - 1
    wo = wo_ref[...]
    g1 = g1_ref[0, 0]

    def mix(a_t):
        return lax.dot_general(a_t, wo, (((0,), (0,)), ((), ())), preferred_element_type=F32)

    a_main = jnp.concatenate([a_ref[0, cc] for cc in range(a_ref.shape[1])], axis=1)
    x1 = x_ref[0] + g1 * mix(a_main)
    x1p = xp_ref[0] + g1 * mix(ap_ref[0, 0][:, LANES - HALO:])
    x1n = xn_ref[0] + g1 * mix(an_ref[0, 0][:, :HALO])

    g = g_ref[...]
    sc = sc_ref[0, 0]
    sh = sh_ref[0, 0]
    hp = jnp.where(i > 0, _norm_mod(x1p, g, sc, sh), 0.0)
    hn = jnp.where(i < last, _norm_mod(x1n, g, sc, sh), 0.0)
    h_scr[0:2 * HALO] = jnp.concatenate([hn, hp], axis=0).astype(BF16)
    h_scr[2 * HALO:] = _norm_mod(x1, g, sc, sh).astype(BF16)
    h = h_scr[...]
    ext = tm + 2 * HALO

    def conv(u, off):
        w0 = cw_ref[0, 0:1, off:off + FFN_CHUNK]
        w1 = cw_ref[0, 1:2, off:off + FFN_CHUNK]
        w2 = cw_ref[0, 2:3, off:off + FFN_CHUNK]
        prev = pltpu.roll(u, 1, axis=0)
        nxt = pltpu.roll(u, ext - 1, axis=0)
        out = w0 * prev + w1 * u + w2 * nxt + cb_ref[0, :, off:off + FFN_CHUNK]
        return out[2 * HALO:]

    for f in range(D_FF // FFN_CHUNK):
        og = f * FFN_CHUNK
        ov = D_FF + f * FFN_CHUNK
        ug = conv(jnp.dot(h, wup_ref[0, :, og:og + FFN_CHUNK], preferred_element_type=F32), og)
        uv = conv(jnp.dot(h, wup_ref[0, :, ov:ov + FFN_CHUNK], preferred_element_type=F32), ov)
        a_scr[:, og:og + FFN_CHUNK] = (ug * jax.nn.sigmoid(ug) * uv).astype(BF16)

    y = jnp.dot(a_scr[...], wdn_ref[0], preferred_element_type=F32)
    out = x1 + gate_ref[0, 0] * y
    if final:
        ms = jnp.mean(out * out, axis=-1, keepdims=True)
        out = out * lax.rsqrt(ms + EPS) * fg_ref[...]
    o_ref[0] = out


def _mix_ffn(a_t, w_o, x, g, mod4, layer, w_up, conv_w, conv_b, w_down, final_g, *, tm, chunk, final):
    bsz, s, d = x.shape
    kdim = w_o.shape[0]
    nc = s // chunk
    cpt = tm // chunk
    hb = tm // HALO
    nh = s // HALO
    return pl.pallas_call(
        functools.partial(_mix_ffn_kernel, tm=tm, final=final),
        grid=(bsz, s // tm),
        in_specs=[pl.BlockSpec((1, cpt, kdim, chunk), lambda b, i: (b, i, 0, 0)),
                  pl.BlockSpec((1, 1, kdim, LANES),
                               lambda b, i: (b, jnp.maximum(i * cpt - 1, 0), 0, chunk // LANES - 1)),
                  pl.BlockSpec((1, 1, kdim, LANES),
                               lambda b, i: (b, jnp.minimum((i + 1) * cpt, nc - 1), 0, 0)),
                  pl.BlockSpec((1, tm, d), lambda b, i: (b, i, 0)),
                  pl.BlockSpec((1, HALO, d), lambda b, i: (b, jnp.maximum(i * hb - 1, 0), 0)),
                  pl.BlockSpec((1, HALO, d), lambda b, i: (b, jnp.minimum((i + 1) * hb, nh - 1), 0)),
                  _resident((kdim, d)),
                  pl.BlockSpec((1, 1, 1, d), lambda b, i: (b, 2, 0, 0)),
                  _resident((1, d)),
                  pl.BlockSpec((1, 1, 1, d), lambda b, i: (b, 4, 0, 0)),
                  pl.BlockSpec((1, 1, 1, d), lambda b, i: (b, 3, 0, 0)),
                  pl.BlockSpec((1, 1, 1, d), lambda b, i: (b, 5, 0, 0)),
                  _resident((d, 2 * D_FF), layer),
                  _resident((3, 2 * D_FF), layer),
                  _resident((1, 2 * D_FF), layer),
                  _resident((D_FF, d), layer),
                  _resident((1, d))],
        out_specs=pl.BlockSpec((1, tm, d), lambda b, i: (b, i, 0)),
        out_shape=jax.ShapeDtypeStruct((bsz, s, d), F32),
        scratch_shapes=[pltpu.VMEM((tm + 2 * HALO, d), BF16),
                        pltpu.VMEM((tm, D_FF), BF16)],
        compiler_params=_cparams(2),
        name="mix_ffn_final" if final else "mix_ffn",
    )(a_t, a_t, a_t, x, x, x, w_o, mod4, g.reshape(1, d), mod4, mod4, mod4, w_up, conv_w,
      conv_b.reshape(DEPTH, 1, -1), w_down, final_g.reshape(1, d))


def _t5_bucket(rel):
    nb = NUM_BUCKETS // 2
    max_exact = nb // 2
    ret = jnp.where(rel > 0, nb, 0)
    n = jnp.abs(rel)
    nf = jnp.maximum(n, 1).astype(jnp.float32)
    large = max_exact + (jnp.log(nf / max_exact) / math.log(MAX_DISTANCE / max_exact)
                         * (nb - max_exact)).astype(jnp.int32)
    large = jnp.minimum(large, nb - 1)
    return ret + jnp.where(n < max_exact, n, large)


N_EDGE = 3


def _bias_kernel(bucket_ref, rb_ref, o_ref):
    hh = pl.program_id(0)
    head = (hh % A_KV) * A_GROUP + hh // A_KV
    bucket = bucket_ref[...]
    c = lax.broadcasted_iota(jnp.int32, bucket.shape, 0)
    q = lax.broadcasted_iota(jnp.int32, bucket.shape, 1)
    band = jnp.abs(c - WINDOW - q) <= WINDOW
    val = jnp.zeros(bucket.shape, F32)
    for kb in range(NUM_BUCKETS):
        val = jnp.where(bucket == kb, rb_ref[kb, head], val)
    val = val * LOG2E
    o_ref[0, 0] = jnp.where(band & (c >= WINDOW), val, -jnp.inf)
    o_ref[1, 0] = jnp.where(band, val, -jnp.inf)
    o_ref[N_EDGE - 1, 0] = jnp.where(band & (c < WINDOW + A_BLOCK), val, -jnp.inf)


def _bias_table(rel_bias):
    c_idx = jnp.arange(A_SPAN)[:, None]
    q_idx = jnp.arange(A_BLOCK)[None, :]
    bucket = _t5_bucket(c_idx - WINDOW - q_idx).astype(jnp.int32)
    return pl.pallas_call(
        _bias_kernel,
        grid=(A_HEADS,),
        in_specs=[pl.BlockSpec((A_SPAN, A_BLOCK), lambda h: (0, 0)),
                  pl.BlockSpec(memory_space=pltpu.SMEM)],
        out_specs=pl.BlockSpec((N_EDGE, 1, A_SPAN, A_BLOCK), lambda h: (0, h, 0, 0)),
        out_shape=jax.ShapeDtypeStruct((N_EDGE, A_HEADS, A_SPAN, A_BLOCK), F32),
        compiler_params=_cparams(1),
        name="bias_table",
    )(bucket, rel_bias)


def _swa_kernel(qt_ref, kp_ref, kc_ref, kn_ref, vp_ref, vc_ref, vn_ref, bias_ref, sink_ref, o_ref):
    j = pl.program_id(1)
    last = pl.num_programs(1) - 1
    n_sub = SWA_SUB
    kblk = ([kp_ref[0, 0]] + [kc_ref[0, 0, u * A_BLOCK:(u + 1) * A_BLOCK] for u in range(n_sub)]
            + [kn_ref[0, 0]])
    vblk = [vp_ref[0, 0]] + [vc_ref[0, u] for u in range(n_sub)] + [vn_ref[0, 0]]
    var = [1] * n_sub
    var[0] = jnp.where(j == 0, 0, 1)
    var[-1] = jnp.where(j == last, N_EDGE - 1, 1)
    lane_kv = lax.broadcasted_iota(jnp.int32, (A_SPAN, A_KV * A_HD), 1) // A_HD
    row_kv = lax.broadcasted_iota(jnp.int32, (A_KV * A_HD, A_SPAN), 0) // A_HD
    grp = lax.broadcasted_iota(jnp.int32, (1, A_GROUP * A_BLOCK), 1) // A_BLOCK
    groups = range(A_GROUP)
    kcat = []
    vstack = []
    qp = []
    for u in range(n_sub):
        kcat.append(jnp.concatenate(kblk[u:u + 3], axis=0))
        vcat = jnp.concatenate(vblk[u:u + 3], axis=1)
        vstack.append(jnp.concatenate([jnp.where(row_kv == kv, vcat, jnp.zeros_like(vcat))
                                       for kv in range(A_KV)], axis=1))
        qp.append(jnp.concatenate([qt_ref[0, u, g * MXU_DIM:(g + 1) * MXU_DIM, :] for g in groups],
                                  axis=1))
    ps = [[] for _ in range(n_sub)]
    rden = [[] for _ in range(n_sub)]
    for kv in range(A_KV):
        sink = jnp.zeros((1, A_GROUP * A_BLOCK), F32)
        for g in groups:
            sink = jnp.where(grp == g, sink_ref[kv * A_GROUP + g], sink)
        sink = sink * LOG2E
        for u in range(n_sub):
            bias = jnp.concatenate([bias_ref[var[u], g * A_KV + kv] for g in groups], axis=1)
            k_kv = jnp.where(lane_kv == kv, kcat[u], jnp.zeros_like(kcat[u]))
            s = jnp.dot(k_kv, qp[u], preferred_element_type=F32) + bias
            m = jnp.maximum(jnp.max(s, axis=0, keepdims=True), sink)
            p = jnp.exp2(s - m)
            rden[u].append(1.0 / (jnp.sum(p, axis=0, keepdims=True) + jnp.exp2(sink - m)))
            ps[u].append(p.astype(BF16))
    for u in range(n_sub):
        out = jnp.dot(vstack[u], jnp.concatenate(ps[u], axis=0), preferred_element_type=F32)
        out = jnp.concatenate([out[kv * A_HD:(kv + 1) * A_HD] * rden[u][kv] for kv in range(A_KV)], axis=0)
        for g in groups:
            o_ref[0, u, g * MXU_DIM:(g + 1) * MXU_DIM, :] = (
                out[:, g * A_BLOCK:(g + 1) * A_BLOCK].astype(BF16))


def _swa(k_tok, qv_t, bias_tbl, sink):
    bsz, _, s, _ = k_tok.shape
    nb = s // A_BLOCK
    ns = SWA_SUB
    assert nb % ns == 0 and nb >= 2 * ns
    qw = A_HEADS * A_HD
    vrow = qw // MXU_DIM
    prev = lambda j: jnp.maximum(ns * j - 1, 0)
    nxt = lambda j: jnp.minimum(ns * j + ns, nb - 1)
    k_edge = lambda row: pl.BlockSpec((1, 1, A_BLOCK, MXU_DIM), lambda b, j: (b, 0, row(j), 0))
    v_edge = lambda row: pl.BlockSpec((1, 1, MXU_DIM, A_BLOCK), lambda b, j: (b, row(j), vrow, 0))
    return pl.pallas_call(
        _swa_kernel,
        grid=(bsz, nb // ns),
        in_specs=[pl.BlockSpec((1, ns, qw, A_BLOCK), lambda b, j: (b, j, 0, 0)),
                  k_edge(prev), pl.BlockSpec((1, 1, ns * A_BLOCK, MXU_DIM), lambda b, j: (b, 0, j, 0)),
                  k_edge(nxt),
                  v_edge(prev), pl.BlockSpec((1, ns, MXU_DIM, A_BLOCK), lambda b, j: (b, j, vrow, 0)),
                  v_edge(nxt),
                  _resident((N_EDGE, A_HEADS, A_SPAN, A_BLOCK)),
                  pl.BlockSpec(memory_space=pltpu.SMEM)],
        out_specs=pl.BlockSpec((1, ns, qw, A_BLOCK), lambda b, j: (b, j, 0, 0)),
        out_shape=jax.ShapeDtypeStruct((bsz, nb, qw, A_BLOCK), BF16),
        compiler_params=_cparams(2),
        name="swa",
    )(qv_t, k_tok, k_tok, k_tok, qv_t, qv_t, qv_t, bias_tbl, sink)


def _prep_weights(mlstm_w_in, mlstm_b_gate, mlstm_w_out, attn_w_in, attn_w_out, ffn_w_up, ffn_w_down):
    qk = M_HEADS * M_DK
    w_m = mlstm_w_in[0]
    w_q = w_m[:, :qk] * (M_DK ** -0.5)
    perm = lambda g: g.reshape(-1, 2, 2, M_HEADS).swapaxes(1, 2).reshape(-1, N_GATES)
    w_t = jnp.concatenate([w_q, w_m[:, 2 * qk:4 * qk], perm(w_m[:, 4 * qk:])], axis=1).T.astype(BF16)
    w_k = w_m[:, qk:2 * qk].astype(BF16)
    b_gate = jnp.broadcast_to(perm(mlstm_b_gate[0][None]).T, (N_GATES, SCAN_CHUNK))
    qw = A_HEADS * A_HD
    kw = A_KV * A_HD
    w_a = attn_w_in[0]
    wq = w_a[:, :qw].reshape(D_MODEL, A_KV, A_GROUP, A_HD).transpose(0, 2, 1, 3).reshape(D_MODEL, qw)
    w_attn_k = w_a[:, qw:qw + kw].astype(BF16)
    w_attn_t = jnp.concatenate([wq * (A_HD ** -0.5 * LOG2E), w_a[:, qw + kw:]], axis=1).T.astype(BF16)
    wo_attn = attn_w_out[0].reshape(A_KV, A_GROUP, A_HD, D_MODEL).transpose(1, 0, 2, 3)
    wo_attn = wo_attn.reshape(qw, D_MODEL).astype(BF16)
    return dict(w_k=w_k, w_t=w_t, b_gate=b_gate,
                wo_m=mlstm_w_out[0].astype(BF16), w_attn_k=w_attn_k, w_attn_t=w_attn_t, wo_attn=wo_attn,
                w_up=ffn_w_up.astype(BF16), w_down=ffn_w_down.astype(BF16))


def _trunk(x, mod, wts, bias_tbl, norm_g, mlstm_head_g, attn_sink, ffn_conv_w, ffn_conv_b, final_g,
           *, tm, tm_ffn):
    bsz, s, d = x.shape
    mod4 = mod[0].reshape(bsz, 6, 1, d)
    k_tok, qvo_t, gates = _inproj_split(x, norm_g[0, 0], mod4, wts["w_k"], wts["w_t"],
                                        tm=tm, chunk=SCAN_CHUNK, tn=512, n_extra=N_GATES)
    prep = _gateprep(gates, wts["b_gate"], SCAN_CHUNK)
    a_t = _mlstm_scan(k_tok, qvo_t, prep, mlstm_head_g[0], SCAN_CHUNK)
    x = _mix_ffn(a_t, wts["wo_m"], x, norm_g[0, 1], mod4, 0, wts["w_up"], ffn_conv_w, ffn_conv_b,
                 wts["w_down"], final_g, tm=tm_ffn, chunk=SCAN_CHUNK, final=False)
    mod4 = mod[1].reshape(bsz, 6, 1, d)
    k_tok, qv_t = _inproj_split(x, norm_g[1, 0], mod4, wts["w_attn_k"], wts["w_attn_t"],
                                tm=tm, chunk=A_BLOCK, tn=640)
    a_t = _swa(k_tok, qv_t, bias_tbl, attn_sink[0])
    return _mix_ffn(a_t, wts["wo_attn"], x, norm_g[1, 1], mod4, 1, wts["w_up"], ffn_conv_w,
                    ffn_conv_b, wts["w_down"], final_g, tm=tm_ffn, chunk=A_BLOCK, final=True)


def kernel(x_prompt, x_sample, c_prompt, c_sample, adaln_w, adaln_b, norm_g, mlstm_w_in, mlstm_b_gate, mlstm_head_g, mlstm_w_out, attn_w_in, attn_sink, attn_w_out, rel_bias, ffn_w_up, ffn_conv_w, ffn_conv_b, ffn_w_down, final_g):
    nbp = x_prompt.shape[0]
    mod = _adaln(jnp.concatenate([c_prompt, c_sample], axis=0), adaln_w, adaln_b)
    wts = _prep_weights(mlstm_w_in, mlstm_b_gate, mlstm_w_out, attn_w_in, attn_w_out, ffn_w_up, ffn_w_down)
    bias_tbl = _bias_table(rel_bias)
    run = functools.partial(_trunk, wts=wts, bias_tbl=bias_tbl, norm_g=norm_g,
                            mlstm_head_g=mlstm_head_g, attn_sink=attn_sink, ffn_conv_w=ffn_conv_w,
                            ffn_conv_b=ffn_conv_b, final_g=final_g, tm=1024, tm_ffn=1024)
    return run(x_prompt, mod[:, :nbp]), run(x_sample, mod[:, nbp:])
```
